```python
import math, functools
import jax, jax.numpy as jnp
from jax import lax
import numpy as np

D_MODEL = 1024
BATCH = 4
SEQ = 4096
DEPTH = 4
DEC_BATCH = 128
DEC_SEQ = 8
PAST_LEN = 8192
PAGE_SIZE = 128

EPS = 1e-6
N_MEM = 256
POOL_WINDOWS = (2, 4, 8, 16)
POOL_GROUPS = 4
POOL_GW = 128
POOL_W = POOL_GROUPS * POOL_GW
POOL_BUF = max(POOL_WINDOWS) - 1
MLA_HEADS = 8
QK_NOPE = 64
QK_ROPE = 32
V_HEAD = 64
Q_LORA = 256
KV_LORA = 128
ROPE_THETA = 10000.0
Q_BLOCK = 128
CHUNK = 128
SG_GROUPS = 4
SG_GW = 128
SG_W = SG_GROUPS * SG_GW
MEM_HEADS = 4
MEM_HD = 128
MEM_W = MEM_HEADS * MEM_HD
N_BRANCH = 4
BRANCH_W = 512
OFF_QA = POOL_W
OFF_KV = OFF_QA + Q_LORA
OFF_C = OFF_KV + KV_LORA + QK_ROPE
OFF_MQ = OFF_C + 2 * SG_W
OFF_G = OFF_MQ + MEM_W
IN_COLS = OFF_G + N_BRANCH * D_MODEL
IN_SPLITS = (OFF_QA, OFF_KV, OFF_C, OFF_MQ, OFF_G)
N_GROUPS = 4
EXP_PER_GROUP = 4
N_EXPERTS = N_GROUPS * EXP_PER_GROUP
TOP_K = 2
D_EXPERT = 256

kernel_name = 'hybrid_pool_mla_sgu_mem_hmoe_step'


def rmsnorm(x, g):
    xf = x.astype(jnp.float32)
    y = xf * lax.rsqrt(jnp.mean(xf * xf, -1, keepdims=True) + EPS)
    return (y * g.astype(jnp.float32)).astype(x.dtype)


def rope(x, pos):
    half = QK_ROPE // 2
    inv = ROPE_THETA ** (-jnp.arange(half, dtype=jnp.float32) / half)
    ang = pos.astype(jnp.float32)[:, None] * inv
    ang = ang.reshape(ang.shape[:1] + (1,) * (x.ndim - 3) + (half,))
    cos, sin = jnp.cos(ang), jnp.sin(ang)
    xf = x.astype(jnp.float32)
    x1, x2 = xf[..., :half], xf[..., half:]
    return jnp.concatenate([x1 * cos - x2 * sin, x1 * sin + x2 * cos], -1).astype(x.dtype)


def attend(q, k, v, mask):
    s = jnp.einsum('bqhd,bkhd->bhqk', q, k).astype(jnp.float32) * (1.0 / math.sqrt(q.shape[-1]))
    if mask is not None:
        s = jnp.where(mask, s, -jnp.inf)
    pr = jax.nn.softmax(s, -1).astype(v.dtype)
    return jnp.einsum('bhqk,bkhd->bqhd', pr, v)


def pool_mix(a, buf, pos, p):
    B, S, C = a.shape
    P = buf.shape[1]
    full = jnp.concatenate([buf, a], 1)
    cs = jnp.cumsum(full.astype(jnp.float32), 1)
    cs = jnp.concatenate([jnp.zeros((B, 1, C), jnp.float32), cs], 1)
    hi = cs[:, P + 1:]
    means = []
    for g, w in enumerate(POOL_WINDOWS):
        sl = slice(g * POOL_GW, (g + 1) * POOL_GW)
        lo = cs[:, P + 1 - w:P + 1 - w + S, sl]
        cnt = jnp.minimum(pos + 1, w).astype(jnp.float32)[:, None]
        means.append((hi[..., sl] - lo) / cnt)
    m = (jnp.concatenate(means, -1) - a.astype(jnp.float32)).astype(a.dtype)
    m = m.reshape(B, S, POOL_GROUPS, POOL_GW)
    y = jnp.einsum('bsgc,gcd->bsgd', m, p['w_pool']).reshape(B, S, POOL_W) * p['pool_scale']
    return y, full[:, -POOL_BUF:]


def mla_query(zqa, pos, p):
    B, S, _ = zqa.shape
    q = (rmsnorm(zqa, p['g_qa']) @ p['w_qb']).reshape(B, S, MLA_HEADS, QK_NOPE + QK_ROPE)
    q = jnp.concatenate([q[..., :QK_NOPE], rope(q[..., QK_NOPE:], pos)], -1)
    return rmsnorm(q, p['g_qn'])


def mla_latent(zkv, pos, p):
    c = rmsnorm(zkv[..., :KV_LORA], p['g_kva'])
    kpe = rope(zkv[..., KV_LORA:], pos)
    return c, kpe


def mla_keys_values(c, kpe, p):
    B, K, _ = c.shape
    kv = (c @ p['w_kvb']).reshape(B, K, MLA_HEADS, QK_NOPE + V_HEAD)
    k = jnp.concatenate([kv[..., :QK_NOPE],
                         jnp.broadcast_to(kpe[:, :, None, :], (B, K, MLA_HEADS, QK_ROPE))], -1)
    return rmsnorm(k, p['g_kn']), kv[..., QK_NOPE:]


def mla_prompt(q, c, kpe, pos, p):
    k, v = mla_keys_values(c, kpe, p)
    B, S = q.shape[:2]
    nb = S // Q_BLOCK
    qb = jnp.moveaxis(q.reshape(B, nb, Q_BLOCK, MLA_HEADS, QK_NOPE + QK_ROPE), 1, 0)
    pb = pos.reshape(nb, Q_BLOCK)
    o = lax.map(lambda a: attend(a[0], k, v, pos[None, :] <= a[1][:, None]), (qb, pb))
    return jnp.moveaxis(o, 0, 1).reshape(B, S, MLA_HEADS * V_HEAD)


def mla_sample(q, c_new, kpe_new, cache_ckv, cache_kpe, page_table, layer, pos, p):
    B, T = q.shape[:2]
    past = page_table.shape[1] * PAGE_SIZE
    k_pos = jnp.arange(past + T, dtype=jnp.int32)
    mask = k_pos[None, :] <= pos[:, None]

    def one(a):
        pt, c1, kp1, q1 = a
        c = jnp.concatenate([cache_ckv[layer, pt].reshape(past, KV_LORA), c1], 0)
        kp = jnp.concatenate([cache_kpe[layer, pt].reshape(past, QK_ROPE), kp1], 0)
        k, v = mla_keys_values(c[None], kp[None], p)
        return attend(q1[None], k, v, mask)[0]

    o = lax.map(one, (page_table, c_new, kpe_new, q))
    return o.reshape(B, T, MLA_HEADS * V_HEAD)


def sgu(zc, p):
    B, S, _ = zc.shape
    zc = jax.nn.gelu(zc)
    u, v = zc[..., :SG_W], zc[..., SG_W:]
    v = rmsnorm(v.reshape(B, S, SG_GROUPS, SG_GW), p['g_sgv'])
    n = -(-S // CHUNK)
    vp = jnp.pad(v, ((0, 0), (0, n * CHUNK - S), (0, 0), (0, 0))).reshape(B, n, CHUNK, SG_GROUPS, SG_GW)
    ws = p['w_sp'] * jnp.tril(jnp.ones((CHUNK, CHUNK), p['w_sp'].dtype))
    s = jnp.einsum('gts,bnsgc->bntgc', ws, vp) + jnp.swapaxes(p['b_sp'], 0, 1)[:, :, None]
    s = s.reshape(B, n * CHUNK, SG_GROUPS * SG_GW)[:, :S]
    return u * s, v.reshape(B, S, SG_W)


def memory_kv(mem, p):
    B, M, _ = mem.shape
    mn = rmsnorm(mem, p['g_mem'])
    k = rmsnorm((mn @ p['w_mk']).reshape(B, M, MEM_HEADS, MEM_HD), p['g_mkn'])
    v = (mn @ p['w_mv']).reshape(B, M, MEM_HEADS, MEM_HD)
    return k, v


def memory_attend(zmq, mk, mv, p):
    B, S, _ = zmq.shape
    q = rmsnorm(zmq.reshape(B, S, MEM_HEADS, MEM_HD), p['g_mqn'])
    return attend(q, mk, mv, None).reshape(B, S, MEM_W)


def hier_moe(h, p):
    B, S, _ = h.shape
    gprob = jax.nn.softmax((h @ p['w_rg']).astype(jnp.float32) + p['b_rg'], -1)
    gp, gi = lax.top_k(gprob, 1)
    el = ((h @ p['w_re']).astype(jnp.float32) + p['b_re']).reshape(B, S, N_GROUPS, EXP_PER_GROUP)
    el = jnp.einsum('bsge,bsg->bse', el, jax.nn.one_hot(gi[..., 0], N_GROUPS, dtype=jnp.float32))
    ev, ei = lax.top_k(jax.nn.softmax(el, -1), TOP_K)
    w = gp * ev / jnp.sum(ev, -1, keepdims=True)
    ids = gi * EXP_PER_GROUP + ei
    cw = jnp.sum(jax.nn.one_hot(ids, N_EXPERTS, dtype=jnp.float32) * w[..., None], 2).astype(h.dtype)
    a = jnp.einsum('bsd,edf->bsef', h, p['w_eg'])
    b = jnp.einsum('bsd,edf->bsef', h, p['w_eu'])
    return jnp.einsum('bsef,efd->bsd', jax.nn.silu(a) * b * cw[..., None], p['w_ed'])


def layer_forward(x, pos, pool_buf, mk, mv, mla_fn, p):
    B, S, _ = x.shape
    h = rmsnorm(x, p['g_mix'])
    z = h @ p['w_in']
    za, zqa, zkv, zc, zmq, zg = jnp.split(z, IN_SPLITS, axis=-1)
    y_a, pool_new = pool_mix(za, pool_buf, pos, p)
    q = mla_query(zqa, pos, p)
    c, kpe = mla_latent(zkv, pos, p)
    y_b = mla_fn(q, c, kpe)
    y_c, v_rows = sgu(zc, p)
    y_m = memory_attend(zmq, mk, mv, p)
    ys = jnp.stack([y_a, y_b, y_c, y_m], 2)
    br = jnp.einsum('bsnc,ncd->bsnd', ys, p['w_br'])
    gates = jax.nn.sigmoid(zg.reshape(B, S, N_BRANCH, D_MODEL))
    x = x + jnp.sum(gates * br, 2) @ p['w_o']
    x = x + hier_moe(rmsnorm(x, p['g_ffn']), p)
    return x, c, kpe, pool_new, v_rows


def setup_inputs(seed: int = 0) -> dict:
    key = jax.random.key(seed)
    ks = iter(jax.random.split(key, 48))
    f32 = jnp.float32

    def nrm(shape, scale=1.0):
        return jax.random.normal(next(ks), shape, f32) * scale

    def gain(shape):
        return 1.0 + nrm(shape, 0.02)

    n_pages = PAST_LEN // PAGE_SIZE
    n_used = DEC_BATCH * n_pages
    n_pool = n_used + (n_used + 3) // 4
    page_table = jax.random.permutation(next(ks), n_pool)[:n_used].reshape(DEC_BATCH, n_pages).astype(jnp.int32)
    L = DEPTH
    return {
        'x_prompt': nrm((BATCH, SEQ, D_MODEL)),
        'x_sample': nrm((DEC_BATCH, DEC_SEQ, D_MODEL)),
        'mem_prompt': nrm((BATCH, N_MEM, D_MODEL)),
        'cache_ckv': nrm((L, n_pool, PAGE_SIZE, KV_LORA)),
        'cache_kpe': nrm((L, n_pool, PAGE_SIZE, QK_ROPE)),
        'cache_memk': nrm((L, DEC_BATCH, N_MEM, MEM_HEADS, MEM_HD)),
        'cache_memv': nrm((L, DEC_BATCH, N_MEM, MEM_HEADS, MEM_HD)),
        'state_pool': nrm((L, DEC_BATCH, POOL_BUF, POOL_W)),
        'page_table': page_table,
        'g_mix': gain((L, D_MODEL)),
        'w_in': nrm((L, D_MODEL, IN_COLS), D_MODEL ** -0.5),
        'g_qa': gain((L, Q_LORA)),
        'w_qb': nrm((L, Q_LORA, MLA_HEADS * (QK_NOPE + QK_ROPE)), Q_LORA ** -0.5),
        'g_kva': gain((L, KV_LORA)),
        'w_kvb': nrm((L, KV_LORA, MLA_HEADS * (QK_NOPE + V_HEAD)), KV_LORA ** -0.5),
        'g_qn': gain((L, QK_NOPE + QK_ROPE)),
        'g_kn': gain((L, QK_NOPE + QK_ROPE)),
        'w_pool': nrm((L, POOL_GROUPS, POOL_GW, POOL_GW), POOL_GW ** -0.5),
        'pool_scale': 1.0 + nrm((L, POOL_W), 0.1),
        'g_sgv': gain((L, SG_GROUPS, SG_GW)),
        'w_sp': nrm((L, SG_GROUPS, CHUNK, CHUNK), CHUNK ** -0.5),
        'b_sp': 1.0 + nrm((L, SG_GROUPS, CHUNK), 0.1),
        'g_mem': gain((L, D_MODEL)),
        'w_mk': nrm((L, D_MODEL, MEM_W), D_MODEL ** -0.5),
        'w_mv': nrm((L, D_MODEL, MEM_W), D_MODEL ** -0.5),
        'g_mqn': gain((L, MEM_HD)),
        'g_mkn': gain((L, MEM_HD)),
        'w_br': nrm((L, N_BRANCH, BRANCH_W, D_MODEL), BRANCH_W ** -0.5),
        'w_o': nrm((L, D_MODEL, D_MODEL), D_MODEL ** -0.5),
        'g_ffn': gain((L, D_MODEL)),
        'w_rg': nrm((L, D_MODEL, N_GROUPS), D_MODEL ** -0.5),
        'b_rg': nrm((L, N_GROUPS), 0.01),
        'w_re': nrm((L, D_MODEL, N_EXPERTS), D_MODEL ** -0.5),
        'b_re': nrm((L, N_EXPERTS), 0.01),
        'w_eg': nrm((L, N_EXPERTS, D_MODEL, D_EXPERT), D_MODEL ** -0.5),
        'w_eu': nrm((L, N_EXPERTS, D_MODEL, D_EXPERT), D_MODEL ** -0.5),
        'w_ed': nrm((L, N_EXPERTS, D_EXPERT, D_MODEL), D_EXPERT ** -0.5),
    }


def reference(x_prompt, x_sample, mem_prompt, cache_ckv, cache_kpe, cache_memk, cache_memv, state_pool,
              page_table, g_mix, w_in, g_qa, w_qb, g_kva, w_kvb, g_qn, g_kn, w_pool, pool_scale, g_sgv,
              w_sp, b_sp, g_mem, w_mk, w_mv, g_mqn, g_mkn, w_br, w_o, g_ffn, w_rg, b_rg, w_re, b_re,
              w_eg, w_eu, w_ed):
    B, S, _ = x_prompt.shape
    T = x_sample.shape[1]
    past = page_table.shape[1] * PAGE_SIZE
    pos_p = jnp.arange(S, dtype=jnp.int32)
    pos_s = past + jnp.arange(T, dtype=jnp.int32)
    zero_buf = jnp.zeros((B, POOL_BUF, POOL_W), x_prompt.dtype)
    xp, xs = x_prompt, x_sample
    ckv_p, kpe_p, memk_p, memv_p, pool_p = [], [], [], [], []
    ckv_s, kpe_s, pool_s, sgv_s = [], [], [], []
    for l in range(DEPTH):
        p = dict(g_mix=g_mix[l], w_in=w_in[l], g_qa=g_qa[l], w_qb=w_qb[l], g_kva=g_kva[l], w_kvb=w_kvb[l],
                 g_qn=g_qn[l], g_kn=g_kn[l], w_pool=w_pool[l], pool_scale=pool_scale[l], g_sgv=g_sgv[l],
                 w_sp=w_sp[l], b_sp=b_sp[l], g_mem=g_mem[l], w_mk=w_mk[l], w_mv=w_mv[l], g_mqn=g_mqn[l],
                 g_mkn=g_mkn[l], w_br=w_br[l], w_o=w_o[l], g_ffn=g_ffn[l], w_rg=w_rg[l], b_rg=b_rg[l],
                 w_re=w_re[l], b_re=b_re[l], w_eg=w_eg[l], w_eu=w_eu[l], w_ed=w_ed[l])
        mk, mv = memory_kv(mem_prompt, p)
        xp, c, kp, pb, _ = layer_forward(xp, pos_p, zero_buf, mk, mv,
                                         functools.partial(mla_prompt, pos=pos_p, p=p), p)
        ckv_p.append(c); kpe_p.append(kp); memk_p.append(mk); memv_p.append(mv); pool_p.append(pb)
        xs, c, kp, pb, vr = layer_forward(
            xs, pos_s, state_pool[l], cache_memk[l], cache_memv[l],
            functools.partial(mla_sample, cache_ckv=cache_ckv, cache_kpe=cache_kpe, page_table=page_table,
                              layer=l, pos=pos_s, p=p), p)
        ckv_s.append(c); kpe_s.append(kp); pool_s.append(pb); sgv_s.append(vr)
    new_ckv_prompt = jnp.stack(ckv_p)
    new_kpe_prompt = jnp.stack(kpe_p)
    new_memk_prompt = jnp.stack(memk_p)
    new_memv_prompt = jnp.stack(memv_p)
    new_pool_prompt = jnp.stack(pool_p)
    new_ckv_sample = jnp.stack(ckv_s)
    new_kpe_sample = jnp.stack(kpe_s)
    new_pool_sample = jnp.stack(pool_s)
    new_sgu_v_sample = jnp.stack(sgv_s)
    return (xp, xs, new_ckv_prompt, new_kpe_prompt, new_memk_prompt, new_memv_prompt, new_pool_prompt,
            new_ckv_sample, new_kpe_sample, new_pool_sample, new_sgu_v_sample)
```

```python
import functools
import math

import numpy as np
import jax
import jax.numpy as jnp
from jax import lax
from jax.experimental import pallas as pl
from jax.experimental.pallas import tpu as pltpu

F32 = jnp.float32
BF16 = jnp.bfloat16
EPS = 1e-6

D_MODEL = 1024
PAGE = 128
N_MEM = 256
POOL_WINDOWS = (2, 4, 8, 16)
POOL_GW = 128
POOL_W = 512
POOL_BUF = 15
POOL_PAD = 16
N_HEADS = 8
QK_NOPE = 64
QK_ROPE = 32
QK_DIM = QK_NOPE + QK_ROPE
V_HEAD = 64
HEAD_PAD = 128
Q_LORA = 256
KV_LORA = 128
ROPE_THETA = 10000.0
CHUNK = 128
SG_GROUPS = 4
SG_W = 512
MEM_HEADS = 4
MEM_HD = 128
MEM_W = 512
N_BRANCH = 4
N_GROUPS = 4
EXP_PER_GROUP = 4
N_EXPERTS = 16
D_EXPERT = 256

ZR_W = 2688
ZR_C_BLK = (1024, 0)
ZR_A_BLK = (512, 2)
ZR_MQ_BLK = (512, 3)
ZR_QA_BLK = (256, 8)
ZR_KV_BLK = (384, 6)

VMEM_LIMIT = 56 * 1024 * 1024

_NN = (((1,), (0,)), ((), ()))
_NT = (((1,), (1,)), ((), ()))


def _tile(n, pref):
    return pref if n % pref == 0 else n


def _cparams(*sem):
    return pltpu.CompilerParams(dimension_semantics=sem, vmem_limit_bytes=VMEM_LIMIT)


def _rms(x, g):
    return x * lax.rsqrt(jnp.mean(x * x, axis=-1, keepdims=True) + EPS) * g


def _const_spec(shape):
    nd = len(shape)
    return pl.BlockSpec(shape, lambda *_: (0,) * nd, pipeline_mode=pl.Buffered(1))


def _split(a):
    hi = lax.bitcast_convert_type(lax.bitcast_convert_type(a, jnp.int32) & jnp.int32(-65536), F32)
    return hi.astype(BF16), (a - hi).astype(BF16)


def _mm(a, w_hi, w_lo, hp, dims=_NN):
    dot = lambda x, y: lax.dot_general(x, y, dims, preferred_element_type=F32)
    if not hp:
        return dot(a.astype(BF16), w_hi)
    a_hi, a_lo = _split(a)
    return dot(a_hi, w_hi) + (dot(a_lo, w_hi) + dot(a_hi, w_lo))


def _mm2(a, b, hp, dims=_NN):
    if not hp:
        return lax.dot_general(a.astype(BF16), b.astype(BF16), dims, preferred_element_type=F32)
    b_hi, b_lo = _split(b)
    return _mm(a, b_hi, b_lo, True, dims)


def _inproj_kernel(x_ref, g_ref, wh_ref, wl_ref, z_ref, *, hp, sigmoid):
    h = _rms(x_ref[...], g_ref[...])
    z = _mm(h, wh_ref[...], wl_ref[...], hp)
    z_ref[...] = jax.nn.sigmoid(z) if sigmoid else z


def _inproj(x, g, w_hl, *, hp, sigmoid):
    t = x.shape[0]
    tm = _tile(t, 256)
    n = w_hl[0].shape[1]
    return pl.pallas_call(
        functools.partial(_inproj_kernel, hp=hp, sigmoid=sigmoid),
        grid=(t // tm,),
        in_specs=[pl.BlockSpec((tm, D_MODEL), lambda i: (i, 0)),
                  _const_spec((1, D_MODEL)),
                  _const_spec(w_hl[0].shape), _const_spec(w_hl[1].shape)],
        out_specs=pl.BlockSpec((tm, n), lambda i: (i, 0)),
        out_shape=jax.ShapeDtypeStruct((t, n), F32),
        compiler_params=_cparams("parallel"),
        name="inproj",
    )(x, g, *w_hl)


def _pool_kernel(a_ref, buf_ref, wph_ref, wpl_ref, ps_ref, y_ref, pn_ref, ext_ref, *, nb, ts, pos0, hp):
    s = pl.program_id(1)

    @pl.when(s == 0)
    def _():
        ext_ref[:, 0:POOL_PAD, :] = buf_ref[...]

    a = a_ref[...].reshape(nb, ts, POOL_W)
    ext_ref[:, POOL_PAD:POOL_PAD + ts, :] = a
    pos = pos0 + s * ts + lax.broadcasted_iota(jnp.int32, (1, ts, 1), 1)
    for g, w in enumerate(POOL_WINDOWS):
        sl = slice(g * POOL_GW, (g + 1) * POOL_GW)
        acc = a[:, :, sl]
        for k in range(1, w):
            acc = acc + ext_ref[:, POOL_PAD - k:POOL_PAD - k + ts, sl]
        cnt = jnp.minimum(pos + 1, w).astype(F32)
        m = (acc / cnt - a[:, :, sl]).reshape(nb * ts, POOL_GW)
        y_ref[:, sl] = _mm(m, wph_ref[g], wpl_ref[g], hp) * ps_ref[:, sl]
    tail = ext_ref[:, ts:ts + POOL_PAD, :]
    pn_ref[...] = tail
    ext_ref[:, 0:POOL_PAD, :] = tail


def _pool(zr, buf, wp_hl, ps, *, nseq, seqlen, pos0, hp):
    if seqlen >= 512:
        nb, ts = 1, 512
    else:
        nb, ts = _tile(nseq, 16), seqlen
    n_s = seqlen // ts
    kern = functools.partial(_pool_kernel, nb=nb, ts=ts, pos0=pos0, hp=hp)
    return pl.pallas_call(
        kern,
        grid=(nseq // nb, n_s),
        in_specs=[pl.BlockSpec((nb * ts, POOL_W), lambda b, s: (b * n_s + s, ZR_A_BLK[1])),
                  pl.BlockSpec((nb, POOL_PAD, POOL_W), lambda b, s: (b, 0, 0)),
                  _const_spec(wp_hl[0].shape), _const_spec(wp_hl[1].shape),
                  _const_spec((1, POOL_W))],
        out_specs=[pl.BlockSpec((nb * ts, POOL_W), lambda b, s: (b * n_s + s, 0)),
                   pl.BlockSpec((nb, POOL_PAD, POOL_W), lambda b, s: (b, 0, 0))],
        out_shape=[jax.ShapeDtypeStruct((nseq * seqlen, POOL_W), F32),
                   jax.ShapeDtypeStruct((nseq, POOL_PAD, POOL_W), F32)],
        scratch_shapes=[pltpu.VMEM((nb, POOL_PAD + ts, POOL_W), F32)],
        compiler_params=_cparams("parallel", "arbitrary"),
        name="pool",
    )(zr, buf, *wp_hl, ps)


def _sgu_kernel(zc_ref, g_ref, ws_ref, b_ref, y_ref, v_ref, *, n_chunk, blk, hp):
    z = jax.nn.gelu(zc_ref[...])
    r = lax.broadcasted_iota(jnp.int32, (CHUNK, CHUNK), 0)
    c = lax.broadcasted_iota(jnp.int32, (CHUNK, CHUNK), 1)
    mask = (c <= r) & ((r // blk) == (c // blk))
    for g in range(SG_GROUPS):
        sl = slice(g * CHUNK, (g + 1) * CHUNK)
        vg = _rms(z[:, SG_W + g * CHUNK:SG_W + (g + 1) * CHUNK], g_ref[g:g + 1, :])
        v_ref[:, sl] = vg
        ws = jnp.where(mask, ws_ref[g], 0.0)
        for ch in range(n_chunk):
            rows = slice(ch * CHUNK, (ch + 1) * CHUNK)
            sp = _mm2(ws, vg[rows], hp) + b_ref[:, sl]
            y_ref[rows, sl] = z[rows, sl] * sp


def _sgu(zr, g_sgv, ws, bfull, *, blk, hp):
    t = zr.shape[0]
    rows = _tile(t, 512)
    n_chunk = rows // CHUNK
    return pl.pallas_call(
        functools.partial(_sgu_kernel, n_chunk=n_chunk, blk=blk, hp=hp),
        grid=(t // rows,),
        in_specs=[pl.BlockSpec((rows, ZR_C_BLK[0]), lambda i: (i, ZR_C_BLK[1])),
                  _const_spec(g_sgv.shape),
                  _const_spec(ws.shape),
                  _const_spec(bfull.shape)],
        out_specs=[pl.BlockSpec((rows, SG_W), lambda i: (i, 0))] * 2,
        out_shape=[jax.ShapeDtypeStruct((t, SG_W), F32)] * 2,
        compiler_params=_cparams("parallel"),
        name="sgu",
    )(zr, g_sgv, ws, bfull)


def _mla_prep_kernel(qa_ref, kv_ref, cos_ref, sin_ref, gqa_ref, gkva_ref, wqh_ref, wql_ref, gqn_ref,
                     qpost_ref, wkvh_ref, wkvl_ref, gkn_ref, q_ref, c_ref, kpe_ref, *maybe_kv,
                     want_kv, hp):
    cos = cos_ref[...]
    sin = sin_ref[...]
    qn = _rms(qa_ref[...], gqa_ref[...])
    qq = _mm(qn, wqh_ref[...], wql_ref[...], hp)
    kv = kv_ref[...]
    c = _rms(kv[:, :KV_LORA], gkva_ref[...])
    c_ref[...] = c
    kpe = kv[:, KV_LORA:2 * KV_LORA] * cos + kv[:, 2 * KV_LORA:] * sin
    kpe_ref[...] = kpe
    if want_kv:
        k_ref, v_ref = maybe_kv
        kvx = _mm(c, wkvh_ref[...], wkvl_ref[...], hp)
        v_ref[...] = kvx[:, N_HEADS * HEAD_PAD:]
    nq = N_HEADS * HEAD_PAD
    for h in range(N_HEADS):
        sl = slice(h * HEAD_PAD, (h + 1) * HEAD_PAD)
        qh = qq[:, sl] * cos + qq[:, nq + h * HEAD_PAD:nq + (h + 1) * HEAD_PAD] * sin
        qh = qh * lax.rsqrt(jnp.sum(qh * qh, axis=-1, keepdims=True) * (1.0 / QK_DIM) + EPS)
        q_ref[:, sl] = qh * gqn_ref[:, sl] * qpost_ref[:, sl]
        if want_kv:
            kh = kvx[:, sl] + kpe
            kh = kh * lax.rsqrt(jnp.sum(kh * kh, axis=-1, keepdims=True) * (1.0 / QK_DIM) + EPS)
            k_ref[:, sl] = kh * gkn_ref[:, sl]


def _mla_prep(zr, cos, sin, lw, qpost, *, want_kv, hp):
    t = zr.shape[0]
    tm = _tile(t, 512)
    tab_nblk = cos.shape[0] // tm
    nq = N_HEADS * HEAD_PAD
    in_specs = [pl.BlockSpec((tm, ZR_QA_BLK[0]), lambda i: (i, ZR_QA_BLK[1])),
                pl.BlockSpec((tm, ZR_KV_BLK[0]), lambda i: (i, ZR_KV_BLK[1])),
                pl.BlockSpec((tm, HEAD_PAD), lambda i: (i % tab_nblk, 0)),
                pl.BlockSpec((tm, HEAD_PAD), lambda i: (i % tab_nblk, 0)),
                _const_spec((1, Q_LORA)), _const_spec((1, KV_LORA)),
                _const_spec(lw["wq"][0].shape), _const_spec(lw["wq"][1].shape),
                _const_spec((1, nq)), _const_spec((1, nq)),
                _const_spec(lw["wkv"][0].shape), _const_spec(lw["wkv"][1].shape),
                _const_spec((1, nq))]
    args = [zr, zr, cos, sin, lw["g_qa"], lw["g_kva"], *lw["wq"], lw["g_qn"], qpost, *lw["wkv"], lw["g_kn"]]
    out_specs = [pl.BlockSpec((tm, nq), lambda i: (i, 0)),
                 pl.BlockSpec((tm, KV_LORA), lambda i: (i, 0)),
                 pl.BlockSpec((tm, HEAD_PAD), lambda i: (i, 0))]
    out_shape = [jax.ShapeDtypeStruct((t, nq), F32),
                 jax.ShapeDtypeStruct((t, KV_LORA), F32),
                 jax.ShapeDtypeStruct((t, HEAD_PAD), F32)]
    if want_kv:
        out_specs += [pl.BlockSpec((tm, nq), lambda i: (i, 0)),
                      pl.BlockSpec((tm, N_HEADS * V_HEAD), lambda i: (i, 0))]
        out_shape += [jax.ShapeDtypeStruct((t, nq), F32),
                      jax.ShapeDtypeStruct((t, N_HEADS * V_HEAD), F32)]
    return pl.pallas_call(
        functools.partial(_mla_prep_kernel, want_kv=want_kv, hp=hp),
        grid=(t // tm,),
        in_specs=in_specs, out_specs=out_specs, out_shape=out_shape,
        compiler_params=_cparams("parallel"),
        name="mla_prep",
    )(*args)


def _mla_prompt_kernel(qi_ref, ki_ref, q_ref, k_ref, v_ref, o_ref, m_s, l_s, acc_s, *, tq, hp):
    t = pl.program_id(2)
    qi = qi_ref[t]
    ki = ki_ref[t]

    @pl.when(ki == 0)
    def _():
        m_s[...] = jnp.full(m_s.shape, -jnp.inf, F32)
        l_s[...] = jnp.zeros(l_s.shape, F32)
        acc_s[...] = jnp.zeros(acc_s.shape, F32)

    def step(diagonal):
        v_hi, v_lo = _split(v_ref[...])
        for j in range(2):
            sl = slice(j * HEAD_PAD, (j + 1) * HEAD_PAD)
            s = _mm2(q_ref[:, sl], k_ref[:, sl], hp, _NT)
            if diagonal:
                row = lax.broadcasted_iota(jnp.int32, (tq, tq), 0)
                col = lax.broadcasted_iota(jnp.int32, (tq, tq), 1)
                s = jnp.where(col <= row, s, -jnp.inf)
            m_prev = m_s[j]
            m_new = jnp.maximum(m_prev, jnp.max(s, axis=-1, keepdims=True))
            alpha = jnp.exp(m_prev - m_new)
            p = jnp.exp(s - m_new)
            l_s[j] = alpha * l_s[j] + jnp.sum(p, axis=-1, keepdims=True)
            acc_s[j] = alpha * acc_s[j] + _mm(p, v_hi, v_lo, hp)
            m_s[j] = m_new

    @pl.when(ki < qi)
    def _():
        step(False)

    @pl.when(ki == qi)
    def _():
        step(True)
        lane = lax.broadcasted_iota(jnp.int32, (tq, 2 * V_HEAD), 1)
        o_ref[...] = jnp.where(lane < V_HEAD, acc_s[0] / l_s[0], acc_s[1] / l_s[1])


def _mla_prompt(q, k, v, *, nseq, seqlen, hp):
    tq = _tile(seqlen, 512)
    nq = seqlen // tq
    qi_np = np.array([i for i in range(nq) for _ in range(i + 1)], np.int32)
    ki_np = np.array([j for i in range(nq) for j in range(i + 1)], np.int32)
    n_tri = int(qi_np.shape[0])
    pair_w = 2 * HEAD_PAD
    grid_spec = pltpu.PrefetchScalarGridSpec(
        num_scalar_prefetch=2,
        grid=(nseq, N_HEADS // 2, n_tri),
        in_specs=[pl.BlockSpec((tq, pair_w), lambda b, hp_, t, qi, ki: (b * nq + qi[t], hp_)),
                  pl.BlockSpec((tq, pair_w), lambda b, hp_, t, qi, ki: (b * nq + ki[t], hp_)),
                  pl.BlockSpec((tq, 2 * V_HEAD), lambda b, hp_, t, qi, ki: (b * nq + ki[t], hp_))],
        out_specs=pl.BlockSpec((tq, 2 * V_HEAD), lambda b, hp_, t, qi, ki: (b * nq + qi[t], hp_)),
        scratch_shapes=[pltpu.VMEM((2, tq, 1), F32), pltpu.VMEM((2, tq, 1), F32),
                        pltpu.VMEM((2, tq, 2 * V_HEAD), F32)])
    return pl.pallas_call(
        functools.partial(_mla_prompt_kernel, tq=tq, hp=hp),
        grid_spec=grid_spec,
        out_shape=jax.ShapeDtypeStruct((nseq * seqlen, N_HEADS * V_HEAD), F32),
        compiler_params=_cparams("parallel", "parallel", "arbitrary"),
        name="mla_prompt",
    )(jnp.asarray(qi_np), jnp.asarray(ki_np), q, k, v)


def _mla_sample_kernel(pt_ref, q_ref, *rest, pps, n_new):
    ckv_refs = rest[:pps]
    kpe_refs = rest[pps:2 * pps]
    (cn_ref, kn_ref, wkt_ref, wktp_ref, fold_ref, wv_ref, o_ref,
     lhs_s, qr_s, m_s, l_s, ctx_s) = rest[2 * pps:]
    del pt_ref
    s_idx = pl.program_id(1)
    n_rows = N_HEADS * n_new

    @pl.when(s_idx == 0)
    def _():
        q = q_ref[...]
        lane_head = lax.broadcasted_iota(jnp.int32, q.shape, 1) // HEAD_PAD
        qblk = jnp.concatenate([jnp.where(lane_head == h, q, 0.0) for h in range(N_HEADS)], axis=0)
        qblk = qblk.astype(BF16)
        lhs_s[0:N_HEADS * QK_NOPE, :] = wkt_ref[...]
        lhs_s[N_HEADS * QK_NOPE:, :] = jnp.dot(qblk, wktp_ref[...], preferred_element_type=F32).astype(BF16)
        qr_s[...] = jnp.dot(qblk, fold_ref[...], preferred_element_type=F32).astype(BF16)
        m_s[...] = jnp.full(m_s.shape, -jnp.inf, F32)
        l_s[...] = jnp.zeros(l_s.shape, F32)
        ctx_s[...] = jnp.zeros(ctx_s.shape, F32)

    def process(c, sr, ssr, mask):
        cb = c.astype(BF16)
        big = lax.dot_general(lhs_s[...], cb, _NT, preferred_element_type=F32)
        pieces = []
        for h in range(N_HEADS):
            kn = big[h * QK_NOPE:(h + 1) * QK_NOPE]
            ss = jnp.sum(kn * kn, axis=0, keepdims=True) + ssr
            rk = lax.rsqrt(ss * (1.0 / QK_DIM) + EPS)
            rows = slice(N_HEADS * QK_NOPE + h * n_new, N_HEADS * QK_NOPE + (h + 1) * n_new)
            pieces.append((big[rows] + sr[h * n_new:(h + 1) * n_new]) * rk)
        sc = jnp.concatenate(pieces, axis=0)
        if mask is not None:
            sc = jnp.where(mask, sc, -jnp.inf)
        m_prev = m_s[...]
        m_new = jnp.maximum(m_prev, jnp.max(sc, axis=-1, keepdims=True))
        alpha = jnp.exp(m_prev - m_new)
        p = jnp.exp(sc - m_new)
        l_s[...] = alpha * l_s[...] + jnp.sum(p, axis=-1, keepdims=True)
        ctx_s[...] = alpha * ctx_s[...] + jnp.dot(p.astype(BF16), cb, preferred_element_type=F32)
        m_s[...] = m_new

    kpe_t = jnp.concatenate([r[...] for r in kpe_refs], axis=1)
    process(jnp.concatenate([r[...] for r in ckv_refs], axis=0),
            jnp.dot(qr_s[...], kpe_t.astype(BF16), preferred_element_type=F32),
            jnp.sum(kpe_t * kpe_t, axis=0, keepdims=True), None)

    @pl.when(s_idx == pl.num_programs(1) - 1)
    def _():
        cpad = jnp.concatenate([cn_ref[...], jnp.zeros((PAGE - n_new, KV_LORA), F32)], axis=0)
        kpad = jnp.concatenate([kn_ref[...], jnp.zeros((PAGE - n_new, QK_ROPE), F32)], axis=0)
        key = lax.broadcasted_iota(jnp.int32, (n_rows, PAGE), 1)
        qry = lax.broadcasted_iota(jnp.int32, (n_rows, PAGE), 0) % n_new
        sr = lax.dot_general(qr_s[...], kpad.astype(BF16), _NT, preferred_element_type=F32)
        ones = jnp.ones((8, QK_ROPE), BF16)
        ssr = lax.dot_general(ones, (kpad * kpad).astype(BF16), _NT, preferred_element_type=F32)[0:1]
        process(cpad, sr, ssr, key <= qry)
        ctx = (ctx_s[...] / l_s[...]).astype(BF16)
        res = jnp.dot(ctx, wv_ref[...], preferred_element_type=F32)
        lane_head = lax.broadcasted_iota(jnp.int32, (n_new, N_HEADS * V_HEAD), 1) // V_HEAD
        out = jnp.zeros((n_new, N_HEADS * V_HEAD), F32)
        for h in range(N_HEADS):
            out = out + jnp.where(lane_head == h, res[h * n_new:(h + 1) * n_new], 0.0)
        o_ref[...] = out


def _mla_sample(q, c_new, kpe_new, cache_ckv, cache_kpe_t, page_table, layer, lw):
    nseq, n_new, _ = q.shape
    n_pages = page_table.shape[1]
    pps = _tile(n_pages, 16)
    n_steps = n_pages // pps
    n_rows = N_HEADS * n_new

    def page_spec(shape, i):
        return pl.BlockSpec((None, None) + shape, lambda b, s, pt: (layer, pt[b, s * pps + i], 0, 0))

    in_specs = ([pl.BlockSpec((None, n_new, N_HEADS * HEAD_PAD), lambda b, s, pt: (b, 0, 0))]
                + [page_spec((PAGE, KV_LORA), i) for i in range(pps)]
                + [page_spec((QK_ROPE, PAGE), i) for i in range(pps)]
                + [pl.BlockSpec((None, n_new, KV_LORA), lambda b, s, pt: (b, 0, 0)),
                   pl.BlockSpec((None, n_new, QK_ROPE), lambda b, s, pt: (b, 0, 0)),
                   _const_spec(lw["wkt"].shape), _const_spec(lw["wktp"].shape),
                   _const_spec(lw["fold"].shape), _const_spec(lw["wv"].shape)])
    grid_spec = pltpu.PrefetchScalarGridSpec(
        num_scalar_prefetch=1,
        grid=(nseq, n_steps),
        in_specs=in_specs,
        out_specs=pl.BlockSpec((None, n_new, N_HEADS * V_HEAD), lambda b, s, pt: (b, 0, 0)),
        scratch_shapes=[pltpu.VMEM((N_HEADS * QK_NOPE + n_rows, KV_LORA), BF16),
                        pltpu.VMEM((n_rows, QK_ROPE), BF16),
                        pltpu.VMEM((n_rows, 1), F32), pltpu.VMEM((n_rows, 1), F32),
                        pltpu.VMEM((n_rows, KV_LORA), F32)])
    return pl.pallas_call(
        functools.partial(_mla_sample_kernel, pps=pps, n_new=n_new),
        grid_spec=grid_spec,
        out_shape=jax.ShapeDtypeStruct((nseq, n_new, N_HEADS * V_HEAD), F32),
        compiler_params=_cparams("parallel", "arbitrary"),
        name="mla_sample",
    )(page_table, q, *([cache_ckv] * pps), *([cache_kpe_t] * pps), c_new, kpe_new,
      lw["wkt"], lw["wktp"], lw["fold"], lw["wv"])


def _memkv_kernel(x_ref, g_ref, wh_ref, wl_ref, gk_ref, k_ref, v_ref, *, hp):
    kv = _mm(_rms(x_ref[...], g_ref[...]), wh_ref[...], wl_ref[...], hp)
    for h in range(MEM_HEADS):
        sl = slice(h * MEM_HD, (h + 1) * MEM_HD)
        k_ref[:, sl] = _rms(kv[:, sl], gk_ref[...])
    v_ref[...] = kv[:, MEM_W:]


def _memkv(mem, g_mem, wmkv_hl, g_mkn, *, hp):
    t = mem.shape[0]
    tm = _tile(t, 256)
    return pl.pallas_call(
        functools.partial(_memkv_kernel, hp=hp),
        grid=(t // tm,),
        in_specs=[pl.BlockSpec((tm, D_MODEL), lambda i: (i, 0)),
                  _const_spec((1, D_MODEL)), _const_spec(wmkv_hl[0].shape), _const_spec(wmkv_hl[1].shape),
                  _const_spec((1, MEM_HD))],
        out_specs=[pl.BlockSpec((tm, MEM_W), lambda i: (i, 0))] * 2,
        out_shape=[jax.ShapeDtypeStruct((t, MEM_W), F32)] * 2,
        compiler_params=_cparams("parallel"),
        name="memkv",
    )(mem, g_mem, *wmkv_hl, g_mkn)


def _memattn_kernel(q_ref, k_ref, v_ref, g_ref, o_ref, *, hp):
    scale = 1.0 / math.sqrt(MEM_HD)
    for h in range(MEM_HEADS):
        sl = slice(h * MEM_HD, (h + 1) * MEM_HD)
        q = _rms(q_ref[:, sl], g_ref[...]) * scale
        s = _mm2(q, k_ref[:, sl], hp, _NT)
        p = jnp.exp(s - jnp.max(s, axis=-1, keepdims=True))
        l = jnp.sum(p, axis=-1, keepdims=True)
        o_ref[:, sl] = _mm2(p, v_ref[:, sl], hp) / l


def _memattn(zr, mk, mv, g_mqn, *, nseq, seqlen, kv_index, hp):
    tq = _tile(seqlen, 512)
    nq = seqlen // tq
    lead = (None,) * (mk.ndim - 2)
    return pl.pallas_call(
        functools.partial(_memattn_kernel, hp=hp),
        grid=(nseq, nq),
        in_specs=[pl.BlockSpec((tq, ZR_MQ_BLK[0]), lambda b, i: (b * nq + i, ZR_MQ_BLK[1])),
                  pl.BlockSpec(lead + (N_MEM, MEM_W), lambda b, i: kv_index(b)),
                  pl.BlockSpec(lead + (N_MEM, MEM_W), lambda b, i: kv_index(b)),
                  _const_spec((1, MEM_HD))],
        out_specs=pl.BlockSpec((tq, MEM_W), lambda b, i: (b * nq + i, 0)),
        out_shape=jax.ShapeDtypeStruct((nseq * seqlen, MEM_W), F32),
        compiler_params=_cparams("parallel", "parallel"),
        name="memattn",
    )(zr, mk, mv, g_mqn)


def _merge_kernel(x_ref, ya_ref, yb_ref, yc_ref, ym_ref, zg_ref, wbrh_ref, wbrl_ref, woh_ref, wol_ref,
                  gf_ref, wrh_ref, wrl_ref, br_ref, x1_ref, h2_ref, cw_ref, *, hp):
    acc = None
    for b, y_ref in enumerate((ya_ref, yb_ref, yc_ref, ym_ref)):
        term = zg_ref[:, b * D_MODEL:(b + 1) * D_MODEL] * _mm(y_ref[...], wbrh_ref[b], wbrl_ref[b], hp)
        acc = term if acc is None else acc + term
    x1 = x_ref[...] + _mm(acc, woh_ref[...], wol_ref[...], hp)
    x1_ref[...] = x1
    h2 = _rms(x1, gf_ref[...])
    h2_ref[...] = h2
    logits = _mm(h2, wrh_ref[...], wrl_ref[...], True) + br_ref[...]
    lane = lax.broadcasted_iota(jnp.int32, logits.shape, 1).astype(F32)
    big = 1e9
    is_grp = (lane >= N_EXPERTS) & (lane < N_EXPERTS + N_GROUPS)
    gl = jnp.where(is_grp, logits, -jnp.inf)
    ge = jnp.exp(gl - jnp.max(gl, axis=-1, keepdims=True))
    gprob = ge / jnp.sum(ge, axis=-1, keepdims=True)
    gp = jnp.max(gprob, axis=-1, keepdims=True)
    gi = jnp.min(jnp.where(is_grp & (gprob == gp), lane, big), axis=-1, keepdims=True) - N_EXPERTS
    sel = (lane >= gi * EXP_PER_GROUP) & (lane < (gi + 1) * EXP_PER_GROUP)
    el = jnp.where(sel, logits, -jnp.inf)
    ee = jnp.exp(el - jnp.max(el, axis=-1, keepdims=True))
    eprob = ee / jnp.sum(ee, axis=-1, keepdims=True)
    v1 = jnp.max(eprob, axis=-1, keepdims=True)
    i1 = jnp.min(jnp.where(sel & (eprob == v1), lane, big), axis=-1, keepdims=True)
    rest = jnp.where(sel & (lane != i1), eprob, -1.0)
    v2 = jnp.max(rest, axis=-1, keepdims=True)
    i2 = jnp.min(jnp.where(rest == v2, lane, big), axis=-1, keepdims=True)
    den = v1 + v2
    cw_ref[...] = jnp.where(lane == i1, gp * v1 / den, jnp.where(lane == i2, gp * v2 / den, 0.0))


def _merge(x, ya, yb, yc, ym, zg, lw, *, hp):
    t = x.shape[0]
    tm = _tile(t, 256)
    row = lambda w: pl.BlockSpec((tm, w), lambda i: (i, 0))
    consts = [*lw["wbr"], *lw["wo"], lw["g_ffn"], *lw["wr"], lw["b_r"]]
    return pl.pallas_call(
        functools.partial(_merge_kernel, hp=hp),
        grid=(t // tm,),
        in_specs=[row(D_MODEL), row(POOL_W), row(512), row(SG_W), row(MEM_W), row(N_BRANCH * D_MODEL)]
                 + [_const_spec(c.shape) for c in consts],
        out_specs=[row(D_MODEL), row(D_MODEL), row(128)],
        out_shape=[jax.ShapeDtypeStruct((t, D_MODEL), F32),
                   jax.ShapeDtypeStruct((t, D_MODEL), F32),
                   jax.ShapeDtypeStruct((t, 128), F32)],
        compiler_params=_cparams("parallel"),
        name="merge",
    )(x, ya, yb, yc, ym, zg, *consts)


def _moe_kernel(h_ref, cw_ref, x_ref, wguh_ref, wgul_ref, wdh_ref, wdl_ref, o_ref, acc_s, hh_s, hl_s, *, hp):
    e = pl.program_id(1)

    @pl.when(e == 0)
    def _():
        acc_s[...] = jnp.zeros(acc_s.shape, F32)
        hh_s[...], hl_s[...] = _split(h_ref[...])

    dot = lambda a, b: jnp.dot(a, b, preferred_element_type=F32)
    ab = dot(hh_s[...], wguh_ref[...])
    if hp:
        ab = ab + (dot(hl_s[...], wguh_ref[...]) + dot(hh_s[...], wgul_ref[...]))
    cw = cw_ref[...]
    lane = lax.broadcasted_iota(jnp.int32, cw.shape, 1)
    cwe = jnp.sum(jnp.where(lane == e, cw, 0.0), axis=-1, keepdims=True)
    t = jax.nn.silu(ab[:, :D_EXPERT]) * ab[:, D_EXPERT:] * cwe
    acc_s[...] += _mm(t, wdh_ref[...], wdl_ref[...], hp)

    @pl.when(e == N_EXPERTS - 1)
    def _():
        o_ref[...] = x_ref[...] + acc_s[...]


def _moe(h2, cw, x1, wgu_hl, wd_hl, *, hp):
    t = h2.shape[0]
    tm = _tile(t, 512)
    wgu_spec = pl.BlockSpec((None, D_MODEL, 2 * D_EXPERT), lambda i, e: (e, 0, 0))
    wd_spec = pl.BlockSpec((None, D_EXPERT, D_MODEL), lambda i, e: (e, 0, 0))
    return pl.pallas_call(
        functools.partial(_moe_kernel, hp=hp),
        grid=(t // tm, N_EXPERTS),
        in_specs=[pl.BlockSpec((tm, D_MODEL), lambda i, e: (i, 0)),
                  pl.BlockSpec((tm, 128), lambda i, e: (i, 0)),
                  pl.BlockSpec((tm, D_MODEL), lambda i, e: (i, 0)),
                  wgu_spec, wgu_spec, wd_spec, wd_spec],
        out_specs=pl.BlockSpec((tm, D_MODEL), lambda i, e: (i, 0)),
        out_shape=jax.ShapeDtypeStruct((t, D_MODEL), F32),
        scratch_shapes=[pltpu.VMEM((tm, D_MODEL), F32), pltpu.VMEM((tm, D_MODEL), BF16),
                        pltpu.VMEM((tm, D_MODEL), BF16)],
        compiler_params=_cparams("parallel", "arbitrary"),
        name="moe",
    )(h2, cw, x1, *wgu_hl, *wd_hl)


def _hl(w):
    b = lax.bitcast_convert_type(w, jnp.int32)
    b = (b + jnp.int32(0x7FFF) + ((b >> 16) & 1)) & jnp.int32(-65536)
    hi = lax.bitcast_convert_type(b, F32)
    return hi.astype(BF16), (w - hi).astype(BF16)


def _rope_tables(pos):
    half = QK_ROPE // 2
    inv = ROPE_THETA ** (-jnp.arange(half, dtype=F32) / half)
    ang = pos.astype(F32)[:, None] * inv
    cos, sin = jnp.cos(ang), jnp.sin(ang)
    n = pos.shape[0]
    cos128 = jnp.concatenate([jnp.ones((n, QK_NOPE), F32), cos, cos, jnp.zeros((n, 32), F32)], 1)
    sin128 = jnp.concatenate([jnp.zeros((n, QK_NOPE), F32), sin, sin, jnp.zeros((n, 32), F32)], 1)
    return cos128, sin128


def _fold_matrix():
    f = np.zeros((N_HEADS * HEAD_PAD, QK_ROPE), np.float32)
    for h in range(N_HEADS):
        for j in range(QK_ROPE):
            f[h * HEAD_PAD + QK_NOPE + j, j] = 1.0
    return jnp.asarray(f, BF16)


def _head_pad_gain(g):
    return jnp.tile(jnp.concatenate([g, jnp.zeros((HEAD_PAD - QK_DIM,), F32)]), N_HEADS)[None]


def _layer_weights(l, w):
    half = QK_ROPE // 2
    wi = w["w_in"][l]
    w_a, w_qa = wi[:, 0:512], wi[:, 512:768]
    w_lat, w_rope = wi[:, 768:896], wi[:, 896:928]
    w_c, w_mq, w_g = wi[:, 928:1952], wi[:, 1952:2464], wi[:, 2464:]
    z64 = jnp.zeros((D_MODEL, QK_NOPE), F32)
    z32 = jnp.zeros((D_MODEL, 32), F32)
    rope128 = jnp.concatenate([z64, w_rope, z32], 1)
    ropesw128 = jnp.concatenate([z64, -w_rope[:, half:], w_rope[:, :half], z32], 1)
    lw = {}
    lw["wr_in"] = _hl(jnp.concatenate([w_c, w_a, w_mq, w_qa, w_lat, rope128, ropesw128], 1))
    lw["wg_in"] = _hl(w_g)
    lw["g_mix"] = w["g_mix"][l][None]

    wq = w["w_qb"][l].reshape(Q_LORA, N_HEADS, QK_DIM)
    nope, x1, x2 = wq[..., :QK_NOPE], wq[..., QK_NOPE:QK_NOPE + half], wq[..., QK_NOPE + half:]
    zq = lambda n: jnp.zeros((Q_LORA, N_HEADS, n), F32)
    wq1 = jnp.concatenate([nope, x1, x2, zq(32)], -1).reshape(Q_LORA, N_HEADS * HEAD_PAD)
    wq2 = jnp.concatenate([zq(QK_NOPE), -x2, x1, zq(32)], -1).reshape(Q_LORA, N_HEADS * HEAD_PAD)
    lw["wq"] = _hl(jnp.concatenate([wq1, wq2], 1))
    lw["g_qa"] = w["g_qa"][l][None]
    lw["g_kva"] = w["g_kva"][l][None]
    lw["g_qn"] = _head_pad_gain(w["g_qn"][l])
    lw["g_kn"] = _head_pad_gain(w["g_kn"][l])

    wkv = w["w_kvb"][l].reshape(KV_LORA, N_HEADS, QK_NOPE + V_HEAD)
    kn, vv = wkv[..., :QK_NOPE], wkv[..., QK_NOPE:]
    wk_pad = jnp.concatenate([kn, jnp.zeros((KV_LORA, N_HEADS, HEAD_PAD - QK_NOPE), F32)], -1)
    wk_pad = wk_pad.reshape(KV_LORA, N_HEADS * HEAD_PAD)
    wv = vv.reshape(KV_LORA, N_HEADS * V_HEAD)
    lw["wkv"] = _hl(jnp.concatenate([wk_pad, wv], 1))
    lw["wkt"] = kn.reshape(KV_LORA, N_HEADS * QK_NOPE).T.astype(BF16)
    lw["wktp"] = wk_pad.T.astype(BF16)
    lw["wv"] = wv.astype(BF16)
    lw["fold"] = _fold_matrix()

    lw["w_pool"] = _hl(w["w_pool"][l])
    lw["pool_scale"] = w["pool_scale"][l][None]
    lw["g_sgv"] = w["g_sgv"][l]
    lw["w_sp"] = w["w_sp"][l]
    lw["b_sp"] = w["b_sp"][l]
    lw["g_mem"] = w["g_mem"][l][None]
    lw["wmkv"] = _hl(jnp.concatenate([w["w_mk"][l], w["w_mv"][l]], 1))
    lw["g_mqn"] = w["g_mqn"][l][None]
    lw["g_mkn"] = w["g_mkn"][l][None]
    lw["wbr"] = _hl(w["w_br"][l])
    lw["wo"] = _hl(w["w_o"][l])
    lw["g_ffn"] = w["g_ffn"][l][None]
    lw["wr"] = _hl(jnp.concatenate([w["w_re"][l], w["w_rg"][l],
                                    jnp.zeros((D_MODEL, 128 - N_EXPERTS - N_GROUPS), F32)], 1))
    lw["b_r"] = jnp.concatenate([w["b_re"][l], w["b_rg"][l],
                                 jnp.zeros((128 - N_EXPERTS - N_GROUPS,), F32)])[None]
    lw["wgu"] = _hl(jnp.concatenate([w["w_eg"][l], w["w_eu"][l]], -1))
    lw["wd"] = _hl(w["w_ed"][l])
    return lw


def _sgu_operands(lw, seqlen):
    blk = min(seqlen, CHUNK)
    rep = CHUNK // blk
    ws = jnp.tile(lw["w_sp"][:, :blk, :blk], (1, rep, rep))
    b = jnp.tile(lw["b_sp"][:, :blk], (1, rep))
    bfull = jnp.repeat(b.T, CHUNK, axis=1)
    return ws, bfull, blk


def _group_layer(x, lw, *, nseq, seqlen, pos0, cos, sin, pool_buf, mem_k, mem_v, kv_index,
                 mla_sample_fn, hp):
    zr = _inproj(x, lw["g_mix"], lw["wr_in"], hp=hp, sigmoid=False)
    zg = _inproj(x, lw["g_mix"], lw["wg_in"], hp=hp, sigmoid=True)
    ya, pool_new = _pool(zr, pool_buf, lw["w_pool"], lw["pool_scale"],
                         nseq=nseq, seqlen=seqlen, pos0=pos0, hp=hp)
    ws, bfull, blk = _sgu_operands(lw, seqlen)
    yc, v_rows = _sgu(zr, lw["g_sgv"], ws, bfull, blk=blk, hp=hp)
    scale = 1.0 / math.sqrt(QK_DIM)
    if mla_sample_fn is not None:
        q, c, kpe128 = _mla_prep(zr, cos, sin, lw, lw["g_kn"] * scale, want_kv=False, hp=hp)
        kpe = kpe128[:, QK_NOPE:QK_DIM]
        yb = mla_sample_fn(q, c, kpe)
    else:
        qpost = jnp.full((1, N_HEADS * HEAD_PAD), scale, F32)
        q, c, kpe128, k, v = _mla_prep(zr, cos, sin, lw, qpost, want_kv=True, hp=hp)
        kpe = kpe128[:, QK_NOPE:QK_DIM]
        yb = _mla_prompt(q, k, v, nseq=nseq, seqlen=seqlen, hp=hp)
    ym = _memattn(zr, mem_k, mem_v, lw["g_mqn"], nseq=nseq, seqlen=seqlen, kv_index=kv_index, hp=hp)
    x1, h2, cw = _merge(x, ya, yb, yc, ym, zg, lw, hp=hp)
    x2 = _moe(h2, cw, x1, lw["wgu"], lw["wd"], hp=hp)
    return x2, c, kpe, pool_new, v_rows


def kernel(x_prompt, x_sample, mem_prompt, cache_ckv, cache_kpe, cache_memk, cache_memv, state_pool, page_table, g_mix, w_in, g_qa, w_qb, g_kva, w_kvb, g_qn, g_kn, w_pool, pool_scale, g_sgv, w_sp, b_sp, g_mem, w_mk, w_mv, g_mqn, g_mkn, w_br, w_o, g_ffn, w_rg, b_rg, w_re, b_re, w_eg, w_eu, w_ed):
    w = dict(g_mix=g_mix, w_in=w_in, g_qa=g_qa, w_qb=w_qb, g_kva=g_kva, w_kvb=w_kvb, g_qn=g_qn, g_kn=g_kn,
             w_pool=w_pool, pool_scale=pool_scale, g_sgv=g_sgv, w_sp=w_sp, b_sp=b_sp, g_mem=g_mem,
             w_mk=w_mk, w_mv=w_mv, g_mqn=g_mqn, g_mkn=g_mkn, w_br=w_br, w_o=w_o, g_ffn=g_ffn,
             w_rg=w_rg, b_rg=b_rg, w_re=w_re, b_re=b_re, w_eg=w_eg, w_eu=w_eu, w_ed=w_ed)
    nb, seq, _ = x_prompt.shape
    ndb, t_new, _ = x_sample.shape
    depth = w_in.shape[0]
    past = page_table.shape[1] * PAGE

    cos_p, sin_p = _rope_tables(jnp.arange(seq, dtype=jnp.int32))
    cos_s, sin_s = _rope_tables(past + jnp.arange(t_new, dtype=jnp.int32))
    tm_s = _tile(ndb * t_new, 512)
    cos_s = jnp.tile(cos_s, (tm_s // t_new, 1))
    sin_s = jnp.tile(sin_s, (tm_s // t_new, 1))

    xp = x_prompt.reshape(nb * seq, D_MODEL)
    xs = x_sample.reshape(ndb * t_new, D_MODEL)
    mem = mem_prompt.reshape(nb * N_MEM, D_MODEL)
    zero_buf = jnp.zeros((nb, POOL_PAD, POOL_W), F32)
    memk_s = cache_memk.reshape(depth, ndb, N_MEM, MEM_W)
    memv_s = cache_memv.reshape(depth, ndb, N_MEM, MEM_W)
    cache_kpe_t = jnp.swapaxes(cache_kpe, 2, 3)

    outs = [[] for _ in range(9)]
    for l in range(depth):
        lw = _layer_weights(l, w)
        mk, mv = _memkv(mem, lw["g_mem"], lw["wmkv"], lw["g_mkn"], hp=True)
        xp, c, kp, pb, _ = _group_layer(
            xp, lw, nseq=nb, seqlen=seq, pos0=0, cos=cos_p, sin=sin_p, pool_buf=zero_buf,
            mem_k=mk, mem_v=mv, kv_index=lambda b: (b, 0), mla_sample_fn=None, hp=True)
        outs[0].append(c.reshape(nb, seq, KV_LORA))
        outs[1].append(kp.reshape(nb, seq, QK_ROPE))
        outs[2].append(mk.reshape(nb, N_MEM, MEM_HEADS, MEM_HD))
        outs[3].append(mv.reshape(nb, N_MEM, MEM_HEADS, MEM_HD))
        outs[4].append(pb[:, 1:])

        def mla_sample_fn(q, c_new, kpe_new, l=l, lw=lw):
            o = _mla_sample(q.reshape(ndb, t_new, -1), c_new.reshape(ndb, t_new, KV_LORA),
                            kpe_new.reshape(ndb, t_new, QK_ROPE), cache_ckv, cache_kpe_t, page_table, l, lw)
            return o.reshape(ndb * t_new, N_HEADS * V_HEAD)

        buf_s = jnp.pad(state_pool[l], ((0, 0), (1, 0), (0, 0)))
        xs, c, kp, pb, vr = _group_layer(
            xs, lw, nseq=ndb, seqlen=t_new, pos0=past, cos=cos_s, sin=sin_s, pool_buf=buf_s,
            mem_k=memk_s, mem_v=memv_s, kv_index=lambda b, l=l: (l, b, 0, 0),
            mla_sample_fn=mla_sample_fn, hp=False)
        outs[5].append(c.reshape(ndb, t_new, KV_LORA))
        outs[6].append(kp.reshape(ndb, t_new, QK_ROPE))
        outs[7].append(pb[:, 1:])
        outs[8].append(vr.reshape(ndb, t_new, SG_W))

    stacked = [jnp.stack(o) for o in outs]
    return (xp.reshape(nb, seq, D_MODEL), xs.reshape(ndb, t_new, D_MODEL), *stacked)
```

```python
import functools
import math

import numpy as np
import jax
import jax.numpy as jnp
from jax import lax
from jax.experimental import pallas as pl
from jax.experimental.pallas import tpu as pltpu

F32 = jnp.float32
BF16 = jnp.bfloat16
EPS = 1e-6

D_MODEL = 1024
PAGE = 128
N_MEM = 256
POOL_WINDOWS = (2, 4, 8, 16)
POOL_GW = 128
POOL_W = 512
POOL_BUF = 15
POOL_PAD = 16
N_HEADS = 8
QK_NOPE = 64
QK_ROPE = 32
QK_DIM = QK_NOPE + QK_ROPE
V_HEAD = 64
HEAD_PAD = 128
Q_LORA = 256
KV_LORA = 128
ROPE_THETA = 10000.0
CHUNK = 128
SG_GROUPS = 4
SG_W = 512
MEM_HEADS = 4
MEM_HD = 128
MEM_W = 512
N_BRANCH = 4
N_GROUPS = 4
EXP_PER_GROUP = 4
N_EXPERTS = 16
D_EXPERT = 256

ZR_W = 2688
ZR_C_BLK = (1024, 0)
ZR_A_BLK = (512, 2)
ZR_MQ_BLK = (512, 3)
ZR_QA_BLK = (256, 8)
ZR_KV_BLK = (384, 6)

VMEM_LIMIT = 56 * 1024 * 1024

_NN = (((1,), (0,)), ((), ()))
_NT = (((1,), (1,)), ((), ()))


def _tile(n, pref):
    return pref if n % pref == 0 else n


def _cparams(*sem):
    return pltpu.CompilerParams(dimension_semantics=sem, vmem_limit_bytes=VMEM_LIMIT)


def _rms(x, g):
    return x * lax.rsqrt(jnp.mean(x * x, axis=-1, keepdims=True) + EPS) * g


def _const_spec(shape):
    nd = len(shape)
    return pl.BlockSpec(shape, lambda *_: (0,) * nd, pipeline_mode=pl.Buffered(1))


def _split(a):
    hi = lax.bitcast_convert_type(lax.bitcast_convert_type(a, jnp.int32) & jnp.int32(-65536), F32)
    return hi.astype(BF16), (a - hi).astype(BF16)


def _mm(a, w_hi, w_lo, hp, dims=_NN):
    dot = lambda x, y: lax.dot_general(x, y, dims, preferred_element_type=F32)
    if not hp:
        return dot(a.astype(BF16), w_hi)
    a_hi, a_lo = _split(a)
    return dot(a_hi, w_hi) + (dot(a_lo, w_hi) + dot(a_hi, w_lo))


def _mm2(a, b, hp, dims=_NN):
    if not hp:
        return lax.dot_general(a.astype(BF16), b.astype(BF16), dims, preferred_element_type=F32)
    b_hi, b_lo = _split(b)
    return _mm(a, b_hi, b_lo, True, dims)


def _inproj_kernel(x_ref, g_ref, wh_ref, wl_ref, z_ref, *, hp, sigmoid):
    h = _rms(x_ref[...], g_ref[...])
    z = _mm(h, wh_ref[...], wl_ref[...], hp)
    z_ref[...] = jax.nn.sigmoid(z) if sigmoid else z


def _inproj(x, g, w_hl, *, hp, sigmoid):
    t = x.shape[0]
    tm = _tile(t, 256)
    n = w_hl[0].shape[1]
    return pl.pallas_call(
        functools.partial(_inproj_kernel, hp=hp, sigmoid=sigmoid),
        grid=(t // tm,),
        in_specs=[pl.BlockSpec((tm, D_MODEL), lambda i: (i, 0)),
                  _const_spec((1, D_MODEL)),
                  _const_spec(w_hl[0].shape), _const_spec(w_hl[1].shape)],
        out_specs=pl.BlockSpec((tm, n), lambda i: (i, 0)),
        out_shape=jax.ShapeDtypeStruct((t, n), F32),
        compiler_params=_cparams("parallel"),
        name="inproj",
    )(x, g, *w_hl)


def _pool_kernel(a_ref, buf_ref, wph_ref, wpl_ref, ps_ref, y_ref, pn_ref, ext_ref, *, nb, ts, pos0, hp):
    s = pl.program_id(1)

    @pl.when(s == 0)
    def _():
        ext_ref[:, 0:POOL_PAD, :] = buf_ref[...]

    a = a_ref[...].reshape(nb, ts, POOL_W)
    ext_ref[:, POOL_PAD:POOL_PAD + ts, :] = a
    pos = pos0 + s * ts + lax.broadcasted_iota(jnp.int32, (1, ts, 1), 1)
    for g, w in enumerate(POOL_WINDOWS):
        sl = slice(g * POOL_GW, (g + 1) * POOL_GW)
        acc = a[:, :, sl]
        for k in range(1, w):
            acc = acc + ext_ref[:, POOL_PAD - k:POOL_PAD - k + ts, sl]
        cnt = jnp.minimum(pos + 1, w).astype(F32)
        m = (acc / cnt - a[:, :, sl]).reshape(nb * ts, POOL_GW)
        y_ref[:, sl] = _mm(m, wph_ref[g], wpl_ref[g], hp) * ps_ref[:, sl]
    tail = ext_ref[:, ts:ts + POOL_PAD, :]
    pn_ref[...] = tail
    ext_ref[:, 0:POOL_PAD, :] = tail


def _pool(zr, buf, wp_hl, ps, *, nseq, seqlen, pos0, hp):
    if seqlen >= 512:
        nb, ts = 1, 512
    else:
        nb, ts = _tile(nseq, 16), seqlen
    n_s = seqlen // ts
    kern = functools.partial(_pool_kernel, nb=nb, ts=ts, pos0=pos0, hp=hp)
    return pl.pallas_call(
        kern,
        grid=(nseq // nb, n_s),
        in_specs=[pl.BlockSpec((nb * ts, POOL_W), lambda b, s: (b * n_s + s, ZR_A_BLK[1])),
                  pl.BlockSpec((nb, POOL_PAD, POOL_W), lambda b, s: (b, 0, 0)),
                  _const_spec(wp_hl[0].shape), _const_spec(wp_hl[1].shape),
                  _const_spec((1, POOL_W))],
        out_specs=[pl.BlockSpec((nb * ts, POOL_W), lambda b, s: (b * n_s + s, 0)),
                   pl.BlockSpec((nb, POOL_PAD, POOL_W), lambda b, s: (b, 0, 0))],
        out_shape=[jax.ShapeDtypeStruct((nseq * seqlen, POOL_W), F32),
                   jax.ShapeDtypeStruct((nseq, POOL_PAD, POOL_W), F32)],
        scratch_shapes=[pltpu.VMEM((nb, POOL_PAD + ts, POOL_W), F32)],
        compiler_params=_cparams("parallel", "arbitrary"),
        name="pool",
    )(zr, buf, *wp_hl, ps)


def _sgu_kernel(zc_ref, g_ref, ws_ref, b_ref, y_ref, v_ref, *, n_chunk, blk, hp):
    z = jax.nn.gelu(zc_ref[...])
    r = lax.broadcasted_iota(jnp.int32, (CHUNK, CHUNK), 0)
    c = lax.broadcasted_iota(jnp.int32, (CHUNK, CHUNK), 1)
    mask = (c <= r) & ((r // blk) == (c // blk))
    for g in range(SG_GROUPS):
        sl = slice(g * CHUNK, (g + 1) * CHUNK)
        vg = _rms(z[:, SG_W + g * CHUNK:SG_W + (g + 1) * CHUNK], g_ref[g:g + 1, :])
        v_ref[:, sl] = vg
        ws = jnp.where(mask, ws_ref[g], 0.0)
        for ch in range(n_chunk):
            rows = slice(ch * CHUNK, (ch + 1) * CHUNK)
            sp = _mm2(ws, vg[rows], hp) + b_ref[:, sl]
            y_ref[rows, sl] = z[rows, sl] * sp


def _sgu(zr, g_sgv, ws, bfull, *, blk, hp):
    t = zr.shape[0]
    rows = _tile(t, 512)
    n_chunk = rows // CHUNK
    return pl.pallas_call(
        functools.partial(_sgu_kernel, n_chunk=n_chunk, blk=blk, hp=hp),
        grid=(t // rows,),
        in_specs=[pl.BlockSpec((rows, ZR_C_BLK[0]), lambda i: (i, ZR_C_BLK[1])),
                  _const_spec(g_sgv.shape),
                  _const_spec(ws.shape),
                  _const_spec(bfull.shape)],
        out_specs=[pl.BlockSpec((rows, SG_W), lambda i: (i, 0))] * 2,
        out_shape=[jax.ShapeDtypeStruct((t, SG_W), F32)] * 2,
        compiler_params=_cparams("parallel"),
        name="sgu",
    )(zr, g_sgv, ws, bfull)


def _mla_prep_kernel(qa_ref, kv_ref, cos_ref, sin_ref, gqa_ref, gkva_ref, wqh_ref, wql_ref, gqn_ref,
                     qpost_ref, wkvh_ref, wkvl_ref, gkn_ref, q_ref, c_ref, kpe_ref, *maybe_kv,
                     want_kv, hp):
    cos = cos_ref[...]
    sin = sin_ref[...]
    qn = _rms(qa_ref[...], gqa_ref[...])
    qq = _mm(qn, wqh_ref[...], wql_ref[...], hp)
    kv = kv_ref[...]
    c = _rms(kv[:, :KV_LORA], gkva_ref[...])
    c_ref[...] = c
    kpe = kv[:, KV_LORA:2 * KV_LORA] * cos + kv[:, 2 * KV_LORA:] * sin
    kpe_ref[...] = kpe
    nq = N_HEADS * HEAD_PAD

    def put_split(ref, idx, val):
        hi, lo = _split(val)
        ref[:, 2 * idx * HEAD_PAD:(2 * idx + 1) * HEAD_PAD] = hi
        ref[:, (2 * idx + 1) * HEAD_PAD:(2 * idx + 2) * HEAD_PAD] = lo

    if want_kv:
        k_ref, v_ref = maybe_kv
        kvx = _mm(c, wkvh_ref[...], wkvl_ref[...], hp)
        for pair in range(N_HEADS // 2):
            put_split(v_ref, pair, kvx[:, nq + pair * HEAD_PAD:nq + (pair + 1) * HEAD_PAD])
    for h in range(N_HEADS):
        sl = slice(h * HEAD_PAD, (h + 1) * HEAD_PAD)
        qh = qq[:, sl] * cos + qq[:, nq + h * HEAD_PAD:nq + (h + 1) * HEAD_PAD] * sin
        qh = qh * lax.rsqrt(jnp.sum(qh * qh, axis=-1, keepdims=True) * (1.0 / QK_DIM) + EPS)
        qh = qh * gqn_ref[:, sl] * qpost_ref[:, sl]
        if want_kv:
            put_split(q_ref, h, qh)
            kh = kvx[:, sl] + kpe
            kh = kh * lax.rsqrt(jnp.sum(kh * kh, axis=-1, keepdims=True) * (1.0 / QK_DIM) + EPS)
            put_split(k_ref, h, kh * gkn_ref[:, sl])
        else:
            q_ref[:, sl] = qh


def _mla_prep(zr, cos, sin, lw, qpost, *, want_kv, hp):
    t = zr.shape[0]
    tm = _tile(t, 512)
    tab_nblk = cos.shape[0] // tm
    nq = N_HEADS * HEAD_PAD
    in_specs = [pl.BlockSpec((tm, ZR_QA_BLK[0]), lambda i: (i, ZR_QA_BLK[1])),
                pl.BlockSpec((tm, ZR_KV_BLK[0]), lambda i: (i, ZR_KV_BLK[1])),
                pl.BlockSpec((tm, HEAD_PAD), lambda i: (i % tab_nblk, 0)),
                pl.BlockSpec((tm, HEAD_PAD), lambda i: (i % tab_nblk, 0)),
                _const_spec((1, Q_LORA)), _const_spec((1, KV_LORA)),
                _const_spec(lw["wq"][0].shape), _const_spec(lw["wq"][1].shape),
                _const_spec((1, nq)), _const_spec((1, nq)),
                _const_spec(lw["wkv"][0].shape), _const_spec(lw["wkv"][1].shape),
                _const_spec((1, nq))]
    args = [zr, zr, cos, sin, lw["g_qa"], lw["g_kva"], *lw["wq"], lw["g_qn"], qpost, *lw["wkv"], lw["g_kn"]]
    q_w, q_dt = (2 * nq, BF16) if want_kv else (nq, F32)
    out_specs = [pl.BlockSpec((tm, q_w), lambda i: (i, 0)),
                 pl.BlockSpec((tm, KV_LORA), lambda i: (i, 0)),
                 pl.BlockSpec((tm, HEAD_PAD), lambda i: (i, 0))]
    out_shape = [jax.ShapeDtypeStruct((t, q_w), q_dt),
                 jax.ShapeDtypeStruct((t, KV_LORA), F32),
                 jax.ShapeDtypeStruct((t, HEAD_PAD), F32)]
    if want_kv:
        out_specs += [pl.BlockSpec((tm, 2 * nq), lambda i: (i, 0)),
                      pl.BlockSpec((tm, nq), lambda i: (i, 0))]
        out_shape += [jax.ShapeDtypeStruct((t, 2 * nq), BF16),
                      jax.ShapeDtypeStruct((t, nq), BF16)]
    return pl.pallas_call(
        functools.partial(_mla_prep_kernel, want_kv=want_kv, hp=hp),
        grid=(t // tm,),
        in_specs=in_specs, out_specs=out_specs, out_shape=out_shape,
        compiler_params=_cparams("parallel"),
        name="mla_prep",
    )(*args)


def _mla_prompt_kernel(qi_ref, ki_ref, q_ref, k_ref, v_ref, o_ref, m_s, l_s, acc_s, *, tq, sub):
    t = pl.program_id(2)
    qi = qi_ref[t]
    ki = ki_ref[t]

    @pl.when(ki == 0)
    def _():
        m_s[...] = jnp.full(m_s.shape, -jnp.inf, F32)
        l_s[...] = jnp.zeros(l_s.shape, F32)
        acc_s[...] = jnp.zeros(acc_s.shape, F32)

    def step(diagonal):
        for j in range(2):
            base = 2 * j * HEAD_PAD
            for r in range(tq // sub):
                rows = slice(r * sub, (r + 1) * sub)
                n_k = (r + 1) * sub if diagonal else tq
                k_hi = k_ref[0:n_k, base:base + HEAD_PAD]
                k_lo = k_ref[0:n_k, base + HEAD_PAD:base + 2 * HEAD_PAD]
                vv = v_ref[0:n_k, :]
                q2 = q_ref[rows, base:base + 2 * HEAD_PAD]
                s = (lax.dot_general(q2, jnp.concatenate([k_hi, k_hi], axis=1), _NT,
                                     preferred_element_type=F32)
                     + lax.dot_general(q2[:, :HEAD_PAD], k_lo, _NT, preferred_element_type=F32))
                if diagonal:
                    row = r * sub + lax.broadcasted_iota(jnp.int32, (sub, n_k), 0)
                    col = lax.broadcasted_iota(jnp.int32, (sub, n_k), 1)
                    s = jnp.where(col <= row, s, -jnp.inf)
                m_prev = m_s[j, rows]
                m_new = jnp.maximum(m_prev, jnp.max(s, axis=-1, keepdims=True))
                alpha = jnp.exp(m_prev - m_new)
                p = jnp.exp(s - m_new)
                l_s[j, rows] = alpha * l_s[j, rows] + jnp.sum(p, axis=-1, keepdims=True)
                p_hi, p_lo = _split(p)
                pv = jnp.dot(p_hi, vv, preferred_element_type=F32)
                pv = pv[:, :HEAD_PAD] + (pv[:, HEAD_PAD:]
                                         + jnp.dot(p_lo, vv[:, :HEAD_PAD], preferred_element_type=F32))
                acc_s[j, rows] = alpha * acc_s[j, rows] + pv
                m_s[j, rows] = m_new

    @pl.when(ki < qi)
    def _():
        step(False)

    @pl.when(ki == qi)
    def _():
        step(True)
        lane = lax.broadcasted_iota(jnp.int32, (tq, 2 * V_HEAD), 1)
        o_ref[...] = jnp.where(lane < V_HEAD, acc_s[0] / l_s[0], acc_s[1] / l_s[1])


def _mla_prompt(q2, k2, v2, *, nseq, seqlen):
    tq = _tile(seqlen, 512)
    sub = _tile(tq, 256)
    nq = seqlen // tq
    qi_np = np.array([i for i in range(nq) for _ in range(i + 1)], np.int32)
    ki_np = np.array([j for i in range(nq) for j in range(i + 1)], np.int32)
    n_tri = int(qi_np.shape[0])
    pair_w = 4 * HEAD_PAD
    grid_spec = pltpu.PrefetchScalarGridSpec(
        num_scalar_prefetch=2,
        grid=(nseq, N_HEADS // 2, n_tri),
        in_specs=[pl.BlockSpec((tq, pair_w), lambda b, hp_, t, qi, ki: (b * nq + qi[t], hp_)),
                  pl.BlockSpec((tq, pair_w), lambda b, hp_, t, qi, ki: (b * nq + ki[t], hp_)),
                  pl.BlockSpec((tq, 2 * HEAD_PAD), lambda b, hp_, t, qi, ki: (b * nq + ki[t], hp_))],
        out_specs=pl.BlockSpec((tq, 2 * V_HEAD), lambda b, hp_, t, qi, ki: (b * nq + qi[t], hp_)),
        scratch_shapes=[pltpu.VMEM((2, tq, 1), F32), pltpu.VMEM((2, tq, 1), F32),
                        pltpu.VMEM((2, tq, 2 * V_HEAD), F32)])
    return pl.pallas_call(
        functools.partial(_mla_prompt_kernel, tq=tq, sub=sub),
        grid_spec=grid_spec,
        out_shape=jax.ShapeDtypeStruct((nseq * seqlen, N_HEADS * V_HEAD), F32),
        compiler_params=_cparams("parallel", "parallel", "arbitrary"),
        name="mla_prompt",
    )(jnp.asarray(qi_np), jnp.asarray(ki_np), q2, k2, v2)


def _mla_sample_kernel(pt_ref, q_ref, *rest, pps, cpp, n_new):
    ckv_refs = rest[:pps]
    kpe_refs = rest[pps:2 * pps]
    (cn_ref, kn_ref, wkt_ref, wktp_ref, fold_ref, wv_ref, o_ref,
     lhs_s, qr_s, m_s, l_s, ctx_s) = rest[2 * pps:]
    del pt_ref
    s_idx = pl.program_id(1)
    n_rows = N_HEADS * n_new

    @pl.when(s_idx == 0)
    def _():
        q = q_ref[...]
        lane_head = lax.broadcasted_iota(jnp.int32, q.shape, 1) // HEAD_PAD
        qblk = jnp.concatenate([jnp.where(lane_head == h, q, 0.0) for h in range(N_HEADS)], axis=0)
        qblk = qblk.astype(BF16)
        lhs_s[0:N_HEADS * QK_NOPE, :] = wkt_ref[...]
        lhs_s[N_HEADS * QK_NOPE:, :] = jnp.dot(qblk, wktp_ref[...], preferred_element_type=F32).astype(BF16)
        qr_s[...] = jnp.dot(qblk, fold_ref[...], preferred_element_type=F32).astype(BF16)
        m_s[...] = jnp.full(m_s.shape, -jnp.inf, F32)
        l_s[...] = jnp.zeros(l_s.shape, F32)
        ctx_s[...] = jnp.zeros(ctx_s.shape, F32)

    def scores(c, sr, ssr):
        cb = c.astype(BF16)
        big = lax.dot_general(lhs_s[...], cb, _NT, preferred_element_type=F32)
        pieces = []
        for h in range(N_HEADS):
            kn = big[h * QK_NOPE:(h + 1) * QK_NOPE]
            ss = jnp.sum(kn * kn, axis=0, keepdims=True) + ssr
            rk = lax.rsqrt(ss * (1.0 / QK_DIM) + EPS)
            rows = slice(N_HEADS * QK_NOPE + h * n_new, N_HEADS * QK_NOPE + (h + 1) * n_new)
            pieces.append((big[rows] + sr[h * n_new:(h + 1) * n_new]) * rk)
        return cb, jnp.concatenate(pieces, axis=0)

    def update(chunks):
        m_new = m_s[...]
        for _, sc in chunks:
            m_new = jnp.maximum(m_new, jnp.max(sc, axis=-1, keepdims=True))
        alpha = jnp.exp(m_s[...] - m_new)
        l_new = alpha * l_s[...]
        ctx = alpha * ctx_s[...]
        for cb, sc in chunks:
            p = jnp.exp(sc - m_new)
            l_new = l_new + jnp.sum(p, axis=-1, keepdims=True)
            ctx = ctx + jnp.dot(p.astype(BF16), cb, preferred_element_type=F32)
        l_s[...] = l_new
        ctx_s[...] = ctx
        m_s[...] = m_new

    chunks = []
    for j in range(0, pps, cpp):
        kpe_t = jnp.concatenate([r[...] for r in kpe_refs[j:j + cpp]], axis=1)
        chunks.append(scores(jnp.concatenate([r[...] for r in ckv_refs[j:j + cpp]], axis=0),
                             jnp.dot(qr_s[...], kpe_t.astype(BF16), preferred_element_type=F32),
                             jnp.sum(kpe_t * kpe_t, axis=0, keepdims=True)))
    update(chunks)

    @pl.when(s_idx == pl.num_programs(1) - 1)
    def _():
        cpad = jnp.concatenate([cn_ref[...], jnp.zeros((PAGE - n_new, KV_LORA), F32)], axis=0)
        kpad = jnp.concatenate([kn_ref[...], jnp.zeros((PAGE - n_new, QK_ROPE), F32)], axis=0)
        key = lax.broadcasted_iota(jnp.int32, (n_rows, PAGE), 1)
        qry = lax.broadcasted_iota(jnp.int32, (n_rows, PAGE), 0) % n_new
        sr = lax.dot_general(qr_s[...], kpad.astype(BF16), _NT, preferred_element_type=F32)
        ones = jnp.ones((8, QK_ROPE), BF16)
        ssr = lax.dot_general(ones, (kpad * kpad).astype(BF16), _NT, preferred_element_type=F32)[0:1]
        cb, sc = scores(cpad, sr, ssr)
        update([(cb, jnp.where(key <= qry, sc, -jnp.inf))])
        ctx = (ctx_s[...] / l_s[...]).astype(BF16)
        res = jnp.dot(ctx, wv_ref[...], preferred_element_type=F32)
        lane_head = lax.broadcasted_iota(jnp.int32, (n_new, N_HEADS * V_HEAD), 1) // V_HEAD
        out = jnp.zeros((n_new, N_HEADS * V_HEAD), F32)
        for h in range(N_HEADS):
            out = out + jnp.where(lane_head == h, res[h * n_new:(h + 1) * n_new], 0.0)
        o_ref[...] = out


def _mla_sample(q, c_new, kpe_new, cache_ckv, cache_kpe_t, page_table, layer, lw):
    nseq, n_new, _ = q.shape
    n_pages = page_table.shape[1]
    pps = _tile(n_pages, 16)
    n_steps = n_pages // pps
    n_rows = N_HEADS * n_new

    def page_spec(shape, i):
        return pl.BlockSpec((None, None) + shape, lambda b, s, pt: (layer, pt[b, s * pps + i], 0, 0))

    in_specs = ([pl.BlockSpec((None, n_new, N_HEADS * HEAD_PAD), lambda b, s, pt: (b, 0, 0))]
                + [page_spec((PAGE, KV_LORA), i) for i in range(pps)]
                + [page_spec((QK_ROPE, PAGE), i) for i in range(pps)]
                + [pl.BlockSpec((None, n_new, KV_LORA), lambda b, s, pt: (b, 0, 0)),
                   pl.BlockSpec((None, n_new, QK_ROPE), lambda b, s, pt: (b, 0, 0)),
                   _const_spec(lw["wkt"].shape), _const_spec(lw["wktp"].shape),
                   _const_spec(lw["fold"].shape), _const_spec(lw["wv"].shape)])
    grid_spec = pltpu.PrefetchScalarGridSpec(
        num_scalar_prefetch=1,
        grid=(nseq, n_steps),
        in_specs=in_specs,
        out_specs=pl.BlockSpec((None, n_new, N_HEADS * V_HEAD), lambda b, s, pt: (b, 0, 0)),
        scratch_shapes=[pltpu.VMEM((N_HEADS * QK_NOPE + n_rows, KV_LORA), BF16),
                        pltpu.VMEM((n_rows, QK_ROPE), BF16),
                        pltpu.VMEM((n_rows, 1), F32), pltpu.VMEM((n_rows, 1), F32),
                        pltpu.VMEM((n_rows, KV_LORA), F32)])
    return pl.pallas_call(
        functools.partial(_mla_sample_kernel, pps=pps, cpp=_tile(pps, 4), n_new=n_new),
        grid_spec=grid_spec,
        out_shape=jax.ShapeDtypeStruct((nseq, n_new, N_HEADS * V_HEAD), F32),
        compiler_params=_cparams("parallel", "arbitrary"),
        name="mla_sample",
    )(page_table, q, *([cache_ckv] * pps), *([cache_kpe_t] * pps), c_new, kpe_new,
      lw["wkt"], lw["wktp"], lw["fold"], lw["wv"])


def _memkv_kernel(x_ref, g_ref, wh_ref, wl_ref, gk_ref, k_ref, v_ref, *, hp):
    kv = _mm(_rms(x_ref[...], g_ref[...]), wh_ref[...], wl_ref[...], hp)
    for h in range(MEM_HEADS):
        sl = slice(h * MEM_HD, (h + 1) * MEM_HD)
        k_ref[:, sl] = _rms(kv[:, sl], gk_ref[...])
    v_ref[...] = kv[:, MEM_W:]


def _memkv(mem, g_mem, wmkv_hl, g_mkn, *, hp):
    t = mem.shape[0]
    tm = _tile(t, 256)
    return pl.pallas_call(
        functools.partial(_memkv_kernel, hp=hp),
        grid=(t // tm,),
        in_specs=[pl.BlockSpec((tm, D_MODEL), lambda i: (i, 0)),
                  _const_spec((1, D_MODEL)), _const_spec(wmkv_hl[0].shape), _const_spec(wmkv_hl[1].shape),
                  _const_spec((1, MEM_HD))],
        out_specs=[pl.BlockSpec((tm, MEM_W), lambda i: (i, 0))] * 2,
        out_shape=[jax.ShapeDtypeStruct((t, MEM_W), F32)] * 2,
        compiler_params=_cparams("parallel"),
        name="memkv",
    )(mem, g_mem, *wmkv_hl, g_mkn)


def _memattn_kernel(q_ref, k_ref, v_ref, g_ref, o_ref, *, hp, rows_by_head):
    scale = 1.0 / math.sqrt(MEM_HD)
    for h in range(MEM_HEADS):
        sl = slice(h * MEM_HD, (h + 1) * MEM_HD)
        if rows_by_head:
            k = k_ref[pl.ds(h, N_MEM, stride=MEM_HEADS), :]
            v = v_ref[pl.ds(h, N_MEM, stride=MEM_HEADS), :]
        else:
            k = k_ref[:, sl]
            v = v_ref[:, sl]
        q = _rms(q_ref[:, sl], g_ref[...]) * scale
        s = _mm2(q, k, hp, _NT)
        p = jnp.exp(s - jnp.max(s, axis=-1, keepdims=True))
        l = jnp.sum(p, axis=-1, keepdims=True)
        o_ref[:, sl] = _mm2(p, v, hp) / l


def _memattn(zr, mk, mv, g_mqn, *, nseq, seqlen, kv_index, hp):
    tq = _tile(seqlen, 512)
    nq = seqlen // tq
    lead = (None,) * (mk.ndim - 2)
    kv_blk = (N_MEM, MEM_W) if mk.shape[-1] == MEM_W else (N_MEM * MEM_HEADS, MEM_HD)
    return pl.pallas_call(
        functools.partial(_memattn_kernel, hp=hp, rows_by_head=kv_blk[1] == MEM_HD),
        grid=(nseq, nq),
        in_specs=[pl.BlockSpec((tq, ZR_MQ_BLK[0]), lambda b, i: (b * nq + i, ZR_MQ_BLK[1])),
                  pl.BlockSpec(lead + kv_blk, lambda b, i: kv_index(b)),
                  pl.BlockSpec(lead + kv_blk, lambda b, i: kv_index(b)),
                  _const_spec((1, MEM_HD))],
        out_specs=pl.BlockSpec((tq, MEM_W), lambda b, i: (b * nq + i, 0)),
        out_shape=jax.ShapeDtypeStruct((nseq * seqlen, MEM_W), F32),
        compiler_params=_cparams("parallel", "parallel"),
        name="memattn",
    )(zr, mk, mv, g_mqn)


def _merge_kernel(x_ref, ya_ref, yb_ref, yc_ref, ym_ref, zg_ref, wbrh_ref, wbrl_ref, woh_ref, wol_ref,
                  gf_ref, wrh_ref, wrl_ref, br_ref, x1_ref, h2_ref, cw_ref, *, hp):
    acc = None
    for b, y_ref in enumerate((ya_ref, yb_ref, yc_ref, ym_ref)):
        term = zg_ref[:, b * D_MODEL:(b + 1) * D_MODEL] * _mm(y_ref[...], wbrh_ref[b], wbrl_ref[b], hp)
        acc = term if acc is None else acc + term
    x1 = x_ref[...] + _mm(acc, woh_ref[...], wol_ref[...], hp)
    x1_ref[...] = x1
    h2 = _rms(x1, gf_ref[...])
    h2_ref[...] = h2
    logits = _mm(h2, wrh_ref[...], wrl_ref[...], True) + br_ref[...]
    lane = lax.broadcasted_iota(jnp.int32, logits.shape, 1).astype(F32)
    big = 1e9
    is_grp = (lane >= N_EXPERTS) & (lane < N_EXPERTS + N_GROUPS)
    gl = jnp.where(is_grp, logits, -jnp.inf)
    ge = jnp.exp(gl - jnp.max(gl, axis=-1, keepdims=True))
    gprob = ge / jnp.sum(ge, axis=-1, keepdims=True)
    gp = jnp.max(gprob, axis=-1, keepdims=True)
    gi = jnp.min(jnp.where(is_grp & (gprob == gp), lane, big), axis=-1, keepdims=True) - N_EXPERTS
    sel = (lane >= gi * EXP_PER_GROUP) & (lane < (gi + 1) * EXP_PER_GROUP)
    el = jnp.where(sel, logits, -jnp.inf)
    ee = jnp.exp(el - jnp.max(el, axis=-1, keepdims=True))
    eprob = ee / jnp.sum(ee, axis=-1, keepdims=True)
    v1 = jnp.max(eprob, axis=-1, keepdims=True)
    i1 = jnp.min(jnp.where(sel & (eprob == v1), lane, big), axis=-1, keepdims=True)
    rest = jnp.where(sel & (lane != i1), eprob, -1.0)
    v2 = jnp.max(rest, axis=-1, keepdims=True)
    i2 = jnp.min(jnp.where(rest == v2, lane, big), axis=-1, keepdims=True)
    den = v1 + v2
    cw_ref[...] = jnp.where(lane == i1, gp * v1 / den, jnp.where(lane == i2, gp * v2 / den, 0.0))


def _merge(x, ya, yb, yc, ym, zg, lw, *, hp):
    t = x.shape[0]
    tm = _tile(t, 256)
    row = lambda w: pl.BlockSpec((tm, w), lambda i: (i, 0))
    consts = [*lw["wbr"], *lw["wo"], lw["g_ffn"], *lw["wr"], lw["b_r"]]
    return pl.pallas_call(
        functools.partial(_merge_kernel, hp=hp),
        grid=(t // tm,),
        in_specs=[row(D_MODEL), row(POOL_W), row(512), row(SG_W), row(MEM_W), row(N_BRANCH * D_MODEL)]
                 + [_const_spec(c.shape) for c in consts],
        out_specs=[row(D_MODEL), row(D_MODEL), row(128)],
        out_shape=[jax.ShapeDtypeStruct((t, D_MODEL), F32),
                   jax.ShapeDtypeStruct((t, D_MODEL), F32),
                   jax.ShapeDtypeStruct((t, 128), F32)],
        compiler_params=_cparams("parallel"),
        name="merge",
    )(x, ya, yb, yc, ym, zg, *consts)


def _moe_kernel(h_ref, cw_ref, x_ref, wguh_ref, wgul_ref, wdh_ref, wdl_ref, o_ref, acc_s, hh_s, hl_s, *, hp):
    e = pl.program_id(1)

    @pl.when(e == 0)
    def _():
        acc_s[...] = jnp.zeros(acc_s.shape, F32)
        hh_s[...], hl_s[...] = _split(h_ref[...])

    dot = lambda a, b: jnp.dot(a, b, preferred_element_type=F32)
    ab = dot(hh_s[...], wguh_ref[...])
    if hp:
        ab = ab + (dot(hl_s[...], wguh_ref[...]) + dot(hh_s[...], wgul_ref[...]))
    cw = cw_ref[...]
    lane = lax.broadcasted_iota(jnp.int32, cw.shape, 1)
    cwe = jnp.sum(jnp.where(lane == e, cw, 0.0), axis=-1, keepdims=True)
    t = jax.nn.silu(ab[:, :D_EXPERT]) * ab[:, D_EXPERT:] * cwe
    acc_s[...] += _mm(t, wdh_ref[...], wdl_ref[...], hp)

    @pl.when(e == N_EXPERTS - 1)
    def _():
        o_ref[...] = x_ref[...] + acc_s[...]


def _moe(h2, cw, x1, wgu_hl, wd_hl, *, hp):
    t = h2.shape[0]
    tm = _tile(t, 512)
    wgu_spec = pl.BlockSpec((None, D_MODEL, 2 * D_EXPERT), lambda i, e: (e, 0, 0))
    wd_spec = pl.BlockSpec((None, D_EXPERT, D_MODEL), lambda i, e: (e, 0, 0))
    return pl.pallas_call(
        functools.partial(_moe_kernel, hp=hp),
        grid=(t // tm, N_EXPERTS),
        in_specs=[pl.BlockSpec((tm, D_MODEL), lambda i, e: (i, 0)),
                  pl.BlockSpec((tm, 128), lambda i, e: (i, 0)),
                  pl.BlockSpec((tm, D_MODEL), lambda i, e: (i, 0)),
                  wgu_spec, wgu_spec, wd_spec, wd_spec],
        out_specs=pl.BlockSpec((tm, D_MODEL), lambda i, e: (i, 0)),
        out_shape=jax.ShapeDtypeStruct((t, D_MODEL), F32),
        scratch_shapes=[pltpu.VMEM((tm, D_MODEL), F32), pltpu.VMEM((tm, D_MODEL), BF16),
                        pltpu.VMEM((tm, D_MODEL), BF16)],
        compiler_params=_cparams("parallel", "arbitrary"),
        name="moe",
    )(h2, cw, x1, *wgu_hl, *wd_hl)


def _hl(w):
    b = lax.bitcast_convert_type(w, jnp.int32)
    b = (b + jnp.int32(0x7FFF) + ((b >> 16) & 1)) & jnp.int32(-65536)
    hi = lax.bitcast_convert_type(b, F32)
    return hi.astype(BF16), (w - hi).astype(BF16)


def _rope_tables(pos):
    half = QK_ROPE // 2
    inv = ROPE_THETA ** (-jnp.arange(half, dtype=F32) / half)
    ang = pos.astype(F32)[:, None] * inv
    cos, sin = jnp.cos(ang), jnp.sin(ang)
    n = pos.shape[0]
    cos128 = jnp.concatenate([jnp.ones((n, QK_NOPE), F32), cos, cos, jnp.zeros((n, 32), F32)], 1)
    sin128 = jnp.concatenate([jnp.zeros((n, QK_NOPE), F32), sin, sin, jnp.zeros((n, 32), F32)], 1)
    return cos128, sin128


def _fold_matrix():
    f = np.zeros((N_HEADS * HEAD_PAD, QK_ROPE), np.float32)
    for h in range(N_HEADS):
        for j in range(QK_ROPE):
            f[h * HEAD_PAD + QK_NOPE + j, j] = 1.0
    return jnp.asarray(f, BF16)


def _head_pad_gain(g):
    return jnp.tile(jnp.concatenate([g, jnp.zeros((HEAD_PAD - QK_DIM,), F32)]), N_HEADS)[None]


def _layer_weights(l, w):
    half = QK_ROPE // 2
    wi = w["w_in"][l]
    w_a, w_qa = wi[:, 0:512], wi[:, 512:768]
    w_lat, w_rope = wi[:, 768:896], wi[:, 896:928]
    w_c, w_mq, w_g = wi[:, 928:1952], wi[:, 1952:2464], wi[:, 2464:]
    z64 = jnp.zeros((D_MODEL, QK_NOPE), F32)
    z32 = jnp.zeros((D_MODEL, 32), F32)
    rope128 = jnp.concatenate([z64, w_rope, z32], 1)
    ropesw128 = jnp.concatenate([z64, -w_rope[:, half:], w_rope[:, :half], z32], 1)
    lw = {}
    lw["wr_in"] = _hl(jnp.concatenate([w_c, w_a, w_mq, w_qa, w_lat, rope128, ropesw128], 1))
    lw["wg_in"] = _hl(w_g)
    lw["g_mix"] = w["g_mix"][l][None]

    wq = w["w_qb"][l].reshape(Q_LORA, N_HEADS, QK_DIM)
    nope, x1, x2 = wq[..., :QK_NOPE], wq[..., QK_NOPE:QK_NOPE + half], wq[..., QK_NOPE + half:]
    zq = lambda n: jnp.zeros((Q_LORA, N_HEADS, n), F32)
    wq1 = jnp.concatenate([nope, x1, x2, zq(32)], -1).reshape(Q_LORA, N_HEADS * HEAD_PAD)
    wq2 = jnp.concatenate([zq(QK_NOPE), -x2, x1, zq(32)], -1).reshape(Q_LORA, N_HEADS * HEAD_PAD)
    lw["wq"] = _hl(jnp.concatenate([wq1, wq2], 1))
    lw["g_qa"] = w["g_qa"][l][None]
    lw["g_kva"] = w["g_kva"][l][None]
    lw["g_qn"] = _head_pad_gain(w["g_qn"][l])
    lw["g_kn"] = _head_pad_gain(w["g_kn"][l])

    wkv = w["w_kvb"][l].reshape(KV_LORA, N_HEADS, QK_NOPE + V_HEAD)
    kn, vv = wkv[..., :QK_NOPE], wkv[..., QK_NOPE:]
    wk_pad = jnp.concatenate([kn, jnp.zeros((KV_LORA, N_HEADS, HEAD_PAD - QK_NOPE), F32)], -1)
    wk_pad = wk_pad.reshape(KV_LORA, N_HEADS * HEAD_PAD)
    wv = vv.reshape(KV_LORA, N_HEADS * V_HEAD)
    lw["wkv"] = _hl(jnp.concatenate([wk_pad, wv], 1))
    lw["wkt"] = kn.reshape(KV_LORA, N_HEADS * QK_NOPE).T.astype(BF16)
    lw["wktp"] = wk_pad.T.astype(BF16)
    lw["wv"] = wv.astype(BF16)
    lw["fold"] = _fold_matrix()

    lw["w_pool"] = _hl(w["w_pool"][l])
    lw["pool_scale"] = w["pool_scale"][l][None]
    lw["g_sgv"] = w["g_sgv"][l]
    lw["w_sp"] = w["w_sp"][l]
    lw["b_sp"] = w["b_sp"][l]
    lw["g_mem"] = w["g_mem"][l][None]
    lw["wmkv"] = _hl(jnp.concatenate([w["w_mk"][l], w["w_mv"][l]], 1))
    lw["g_mqn"] = w["g_mqn"][l][None]
    lw["g_mkn"] = w["g_mkn"][l][None]
    lw["wbr"] = _hl(w["w_br"][l])
    lw["wo"] = _hl(w["w_o"][l])
    lw["g_ffn"] = w["g_ffn"][l][None]
    lw["wr"] = _hl(jnp.concatenate([w["w_re"][l], w["w_rg"][l],
                                    jnp.zeros((D_MODEL, 128 - N_EXPERTS - N_GROUPS), F32)], 1))
    lw["b_r"] = jnp.concatenate([w["b_re"][l], w["b_rg"][l],
                                 jnp.zeros((128 - N_EXPERTS - N_GROUPS,), F32)])[None]
    lw["wgu"] = _hl(jnp.concatenate([w["w_eg"][l], w["w_eu"][l]], -1))
    lw["wd"] = _hl(w["w_ed"][l])
    return lw


def _sgu_operands(lw, seqlen):
    blk = min(seqlen, CHUNK)
    rep = CHUNK // blk
    ws = jnp.tile(lw["w_sp"][:, :blk, :blk], (1, rep, rep))
    b = jnp.tile(lw["b_sp"][:, :blk], (1, rep))
    bfull = jnp.repeat(b.T, CHUNK, axis=1)
    return ws, bfull, blk


def _group_layer(x, lw, *, nseq, seqlen, pos0, cos, sin, pool_buf, mem_k, mem_v, kv_index,
                 mla_sample_fn, hp):
    zr = _inproj(x, lw["g_mix"], lw["wr_in"], hp=hp, sigmoid=False)
    zg = _inproj(x, lw["g_mix"], lw["wg_in"], hp=hp, sigmoid=True)
    ya, pool_new = _pool(zr, pool_buf, lw["w_pool"], lw["pool_scale"],
                         nseq=nseq, seqlen=seqlen, pos0=pos0, hp=hp)
    ws, bfull, blk = _sgu_operands(lw, seqlen)
    yc, v_rows = _sgu(zr, lw["g_sgv"], ws, bfull, blk=blk, hp=hp)
    scale = 1.0 / math.sqrt(QK_DIM)
    if mla_sample_fn is not None:
        q, c, kpe128 = _mla_prep(zr, cos, sin, lw, lw["g_kn"] * scale, want_kv=False, hp=hp)
        kpe = kpe128[:, QK_NOPE:QK_DIM]
        yb = mla_sample_fn(q, c, kpe)
    else:
        qpost = jnp.full((1, N_HEADS * HEAD_PAD), scale, F32)
        q, c, kpe128, k, v = _mla_prep(zr, cos, sin, lw, qpost, want_kv=True, hp=hp)
        kpe = kpe128[:, QK_NOPE:QK_DIM]
        yb = _mla_prompt(q, k, v, nseq=nseq, seqlen=seqlen)
    ym = _memattn(zr, mem_k, mem_v, lw["g_mqn"], nseq=nseq, seqlen=seqlen, kv_index=kv_index, hp=hp)
    x1, h2, cw = _merge(x, ya, yb, yc, ym, zg, lw, hp=hp)
    x2 = _moe(h2, cw, x1, lw["wgu"], lw["wd"], hp=hp)
    return x2, c, kpe, pool_new, v_rows


def kernel(x_prompt, x_sample, mem_prompt, cache_ckv, cache_kpe, cache_memk, cache_memv, state_pool, page_table, g_mix, w_in, g_qa, w_qb, g_kva, w_kvb, g_qn, g_kn, w_pool, pool_scale, g_sgv, w_sp, b_sp, g_mem, w_mk, w_mv, g_mqn, g_mkn, w_br, w_o, g_ffn, w_rg, b_rg, w_re, b_re, w_eg, w_eu, w_ed):
    w = dict(g_mix=g_mix, w_in=w_in, g_qa=g_qa, w_qb=w_qb, g_kva=g_kva, w_kvb=w_kvb, g_qn=g_qn, g_kn=g_kn,
             w_pool=w_pool, pool_scale=pool_scale, g_sgv=g_sgv, w_sp=w_sp, b_sp=b_sp, g_mem=g_mem,
             w_mk=w_mk, w_mv=w_mv, g_mqn=g_mqn, g_mkn=g_mkn, w_br=w_br, w_o=w_o, g_ffn=g_ffn,
             w_rg=w_rg, b_rg=b_rg, w_re=w_re, b_re=b_re, w_eg=w_eg, w_eu=w_eu, w_ed=w_ed)
    nb, seq, _ = x_prompt.shape
    ndb, t_new, _ = x_sample.shape
    depth = w_in.shape[0]
    past = page_table.shape[1] * PAGE

    cos_p, sin_p = _rope_tables(jnp.arange(seq, dtype=jnp.int32))
    cos_s, sin_s = _rope_tables(past + jnp.arange(t_new, dtype=jnp.int32))
    tm_s = _tile(ndb * t_new, 512)
    cos_s = jnp.tile(cos_s, (tm_s // t_new, 1))
    sin_s = jnp.tile(sin_s, (tm_s // t_new, 1))

    xp = x_prompt.reshape(nb * seq, D_MODEL)
    xs = x_sample.reshape(ndb * t_new, D_MODEL)
    mem = mem_prompt.reshape(nb * N_MEM, D_MODEL)
    zero_buf = jnp.zeros((nb, POOL_PAD, POOL_W), F32)
    memk_s = cache_memk.reshape(depth, ndb, N_MEM * MEM_HEADS, MEM_HD)
    memv_s = cache_memv.reshape(depth, ndb, N_MEM * MEM_HEADS, MEM_HD)
    cache_kpe_t = jnp.swapaxes(cache_kpe, 2, 3)

    outs = [[] for _ in range(9)]
    for l in range(depth):
        lw = _layer_weights(l, w)
        mk, mv = _memkv(mem, lw["g_mem"], lw["wmkv"], lw["g_mkn"], hp=True)
        xp, c, kp, pb, _ = _group_layer(
            xp, lw, nseq=nb, seqlen=seq, pos0=0, cos=cos_p, sin=sin_p, pool_buf=zero_buf,
            mem_k=mk, mem_v=mv, kv_index=lambda b: (b, 0), mla_sample_fn=None, hp=True)
        outs[0].append(c.reshape(nb, seq, KV_LORA))
        outs[1].append(kp.reshape(nb, seq, QK_ROPE))
        outs[2].append(mk.reshape(nb, N_MEM, MEM_HEADS, MEM_HD))
        outs[3].append(mv.reshape(nb, N_MEM, MEM_HEADS, MEM_HD))
        outs[4].append(pb[:, 1:])

        def mla_sample_fn(q, c_new, kpe_new, l=l, lw=lw):
            o = _mla_sample(q.reshape(ndb, t_new, -1), c_new.reshape(ndb, t_new, KV_LORA),
                            kpe_new.reshape(ndb, t_new, QK_ROPE), cache_ckv, cache_kpe_t, page_table, l, lw)
            return o.reshape(ndb * t_new, N_HEADS * V_HEAD)

        buf_s = jnp.pad(state_pool[l], ((0, 0), (1, 0), (0, 0)))
        xs, c, kp, pb, vr = _group_layer(
            xs, lw, nseq=ndb, seqlen=t_new, pos0=past, cos=cos_s, sin=sin_s, pool_buf=buf_s,
            mem_k=memk_s, mem_v=memv_s, kv_index=lambda b, l=l: (l, b, 0, 0),
            mla_sample_fn=mla_sample_fn, hp=False)
        outs[5].append(c.reshape(ndb, t_new, KV_LORA))
        outs[6].append(kp.reshape(ndb, t_new, QK_ROPE))
        outs[7].append(pb[:, 1:])
        outs[8].append(vr.reshape(ndb, t_new, SG_W))

    stacked = [jnp.stack(o) for o in outs]
    return (xp.reshape(nb, seq, D_MODEL), xs.reshape(ndb, t_new, D_MODEL), *stacked)
```

```python
import functools
import math

import numpy as np
import jax
import jax.numpy as jnp
from jax import lax
from jax.experimental import pallas as pl
from jax.experimental.pallas import tpu as pltpu

F32 = jnp.float32
BF16 = jnp.bfloat16
EPS = 1e-6

D_MODEL = 1024
PAGE = 128
N_MEM = 256
POOL_WINDOWS = (2, 4, 8, 16)
POOL_GW = 128
POOL_W = 512
POOL_BUF = 15
POOL_PAD = 16
N_HEADS = 8
QK_NOPE = 64
QK_ROPE = 32
QK_DIM = QK_NOPE + QK_ROPE
V_HEAD = 64
HEAD_PAD = 128
Q_LORA = 256
KV_LORA = 128
ROPE_THETA = 10000.0
CHUNK = 128
SG_GROUPS = 4
SG_W = 512
MEM_HEADS = 4
MEM_HD = 128
MEM_W = 512
N_BRANCH = 4
N_GROUPS = 4
EXP_PER_GROUP = 4
N_EXPERTS = 16
D_EXPERT = 256

ZR_W = 2688
ZR_C_BLK = (1024, 0)
ZR_A_BLK = (512, 2)
ZR_MQ_BLK = (512, 3)
ZR_QA_BLK = (256, 8)
ZR_KV_BLK = (384, 6)

VMEM_LIMIT = 56 * 1024 * 1024

_NN = (((1,), (0,)), ((), ()))
_NT = (((1,), (1,)), ((), ()))


def _tile(n, pref):
    return pref if n % pref == 0 else n


def _cparams(*sem):
    return pltpu.CompilerParams(dimension_semantics=sem, vmem_limit_bytes=VMEM_LIMIT)


def _rms(x, g):
    return x * lax.rsqrt(jnp.mean(x * x, axis=-1, keepdims=True) + EPS) * g


def _const_spec(shape):
    nd = len(shape)
    return pl.BlockSpec(shape, lambda *_: (0,) * nd, pipeline_mode=pl.Buffered(1))


def _split(a):
    hi = lax.bitcast_convert_type(lax.bitcast_convert_type(a, jnp.int32) & jnp.int32(-65536), F32)
    return hi.astype(BF16), (a - hi).astype(BF16)


def _mm(a, w_hi, w_lo, hp, dims=_NN):
    dot = lambda x, y: lax.dot_general(x, y, dims, preferred_element_type=F32)
    if not hp:
        return dot(a.astype(BF16), w_hi)
    a_hi, a_lo = _split(a)
    return dot(a_hi, w_hi) + (dot(a_lo, w_hi) + dot(a_hi, w_lo))


def _mm2(a, b, hp, dims=_NN):
    if not hp:
        return lax.dot_general(a.astype(BF16), b.astype(BF16), dims, preferred_element_type=F32)
    b_hi, b_lo = _split(b)
    return _mm(a, b_hi, b_lo, True, dims)


def _inproj_kernel(x_ref, g_ref, wh_ref, wl_ref, z_ref, *, hp, sigmoid):
    h = _rms(x_ref[...], g_ref[...])
    z = _mm(h, wh_ref[...], wl_ref[...], hp)
    z_ref[...] = jax.nn.sigmoid(z) if sigmoid else z


def _inproj(x, g, w_hl, *, hp, sigmoid):
    t = x.shape[0]
    tm = _tile(t, 256)
    n = w_hl[0].shape[1]
    return pl.pallas_call(
        functools.partial(_inproj_kernel, hp=hp, sigmoid=sigmoid),
        grid=(t // tm,),
        in_specs=[pl.BlockSpec((tm, D_MODEL), lambda i: (i, 0)),
                  _const_spec((1, D_MODEL)),
                  _const_spec(w_hl[0].shape), _const_spec(w_hl[1].shape)],
        out_specs=pl.BlockSpec((tm, n), lambda i: (i, 0)),
        out_shape=jax.ShapeDtypeStruct((t, n), F32),
        compiler_params=_cparams("parallel"),
        name="inproj",
    )(x, g, *w_hl)


def _pool_kernel(a_ref, buf_ref, wph_ref, wpl_ref, ps_ref, y_ref, pn_ref, ext_ref, *, nb, ts, pos0, hp):
    s = pl.program_id(1)

    @pl.when(s == 0)
    def _():
        ext_ref[:, 0:POOL_PAD, :] = buf_ref[...]

    a = a_ref[...].reshape(nb, ts, POOL_W)
    ext_ref[:, POOL_PAD:POOL_PAD + ts, :] = a
    pos = pos0 + s * ts + lax.broadcasted_iota(jnp.int32, (1, ts, 1), 1)
    for g, w in enumerate(POOL_WINDOWS):
        sl = slice(g * POOL_GW, (g + 1) * POOL_GW)
        acc = a[:, :, sl]
        for k in range(1, w):
            acc = acc + ext_ref[:, POOL_PAD - k:POOL_PAD - k + ts, sl]
        cnt = jnp.minimum(pos + 1, w).astype(F32)
        m = (acc / cnt - a[:, :, sl]).reshape(nb * ts, POOL_GW)
        y_ref[:, sl] = _mm(m, wph_ref[g], wpl_ref[g], hp) * ps_ref[:, sl]
    tail = ext_ref[:, ts:ts + POOL_PAD, :]
    pn_ref[...] = tail
    ext_ref[:, 0:POOL_PAD, :] = tail


def _pool(zr, buf, wp_hl, ps, *, nseq, seqlen, pos0, hp):
    if seqlen >= 512:
        nb, ts = 1, 512
    else:
        nb, ts = _tile(nseq, 16), seqlen
    n_s = seqlen // ts
    kern = functools.partial(_pool_kernel, nb=nb, ts=ts, pos0=pos0, hp=hp)
    return pl.pallas_call(
        kern,
        grid=(nseq // nb, n_s),
        in_specs=[pl.BlockSpec((nb * ts, POOL_W), lambda b, s: (b * n_s + s, ZR_A_BLK[1])),
                  pl.BlockSpec((nb, POOL_PAD, POOL_W), lambda b, s: (b, 0, 0)),
                  _const_spec(wp_hl[0].shape), _const_spec(wp_hl[1].shape),
                  _const_spec((1, POOL_W))],
        out_specs=[pl.BlockSpec((nb * ts, POOL_W), lambda b, s: (b * n_s + s, 0)),
                   pl.BlockSpec((nb, POOL_PAD, POOL_W), lambda b, s: (b, 0, 0))],
        out_shape=[jax.ShapeDtypeStruct((nseq * seqlen, POOL_W), F32),
                   jax.ShapeDtypeStruct((nseq, POOL_PAD, POOL_W), F32)],
        scratch_shapes=[pltpu.VMEM((nb, POOL_PAD + ts, POOL_W), F32)],
        compiler_params=_cparams("parallel", "arbitrary"),
        name="pool",
    )(zr, buf, *wp_hl, ps)


def _sgu_kernel(zc_ref, g_ref, ws_ref, b_ref, y_ref, v_ref, *, n_chunk, blk, hp):
    z = jax.nn.gelu(zc_ref[...])
    r = lax.broadcasted_iota(jnp.int32, (CHUNK, CHUNK), 0)
    c = lax.broadcasted_iota(jnp.int32, (CHUNK, CHUNK), 1)
    mask = (c <= r) & ((r // blk) == (c // blk))
    for g in range(SG_GROUPS):
        sl = slice(g * CHUNK, (g + 1) * CHUNK)
        vg = _rms(z[:, SG_W + g * CHUNK:SG_W + (g + 1) * CHUNK], g_ref[g:g + 1, :])
        v_ref[:, sl] = vg
        ws = jnp.where(mask, ws_ref[g], 0.0)
        for ch in range(n_chunk):
            rows = slice(ch * CHUNK, (ch + 1) * CHUNK)
            sp = _mm2(ws, vg[rows], hp) + b_ref[:, sl]
            y_ref[rows, sl] = z[rows, sl] * sp


def _sgu(zr, g_sgv, ws, bfull, *, blk, hp):
    t = zr.shape[0]
    rows = _tile(t, 512)
    n_chunk = rows // CHUNK
    return pl.pallas_call(
        functools.partial(_sgu_kernel, n_chunk=n_chunk, blk=blk, hp=hp),
        grid=(t // rows,),
        in_specs=[pl.BlockSpec((rows, ZR_C_BLK[0]), lambda i: (i, ZR_C_BLK[1])),
                  _const_spec(g_sgv.shape),
                  _const_spec(ws.shape),
                  _const_spec(bfull.shape)],
        out_specs=[pl.BlockSpec((rows, SG_W), lambda i: (i, 0))] * 2,
        out_shape=[jax.ShapeDtypeStruct((t, SG_W), F32)] * 2,
        compiler_params=_cparams("parallel"),
        name="sgu",
    )(zr, g_sgv, ws, bfull)


def _mla_prep_kernel(qa_ref, kv_ref, cos_ref, sin_ref, gqa_ref, gkva_ref, wqh_ref, wql_ref, gqn_ref,
                     qpost_ref, wkvh_ref, wkvl_ref, gkn_ref, q_ref, c_ref, kpe_ref, *maybe_kv,
                     want_kv, hp):
    cos = cos_ref[...]
    sin = sin_ref[...]
    qn = _rms(qa_ref[...], gqa_ref[...])
    qq = _mm(qn, wqh_ref[...], wql_ref[...], hp)
    kv = kv_ref[...]
    c = _rms(kv[:, :KV_LORA], gkva_ref[...])
    c_ref[...] = c
    kpe = kv[:, KV_LORA:2 * KV_LORA] * cos + kv[:, 2 * KV_LORA:] * sin
    kpe_ref[...] = kpe
    nq = N_HEADS * HEAD_PAD

    def put_split(ref, idx, val):
        hi, lo = _split(val)
        ref[:, 2 * idx * HEAD_PAD:(2 * idx + 1) * HEAD_PAD] = hi
        ref[:, (2 * idx + 1) * HEAD_PAD:(2 * idx + 2) * HEAD_PAD] = lo

    if want_kv:
        k_ref, v_ref = maybe_kv
        kvx = _mm(c, wkvh_ref[...], wkvl_ref[...], hp)
        for pair in range(N_HEADS // 2):
            put_split(v_ref, pair, kvx[:, nq + pair * HEAD_PAD:nq + (pair + 1) * HEAD_PAD])
    for h in range(N_HEADS):
        sl = slice(h * HEAD_PAD, (h + 1) * HEAD_PAD)
        qh = qq[:, sl] * cos + qq[:, nq + h * HEAD_PAD:nq + (h + 1) * HEAD_PAD] * sin
        qh = qh * lax.rsqrt(jnp.sum(qh * qh, axis=-1, keepdims=True) * (1.0 / QK_DIM) + EPS)
        qh = qh * gqn_ref[:, sl] * qpost_ref[:, sl]
        if want_kv:
            put_split(q_ref, h, qh)
            kh = kvx[:, sl] + kpe
            kh = kh * lax.rsqrt(jnp.sum(kh * kh, axis=-1, keepdims=True) * (1.0 / QK_DIM) + EPS)
            put_split(k_ref, h, kh * gkn_ref[:, sl])
        else:
            q_ref[:, sl] = qh


def _mla_prep(zr, cos, sin, lw, qpost, *, want_kv, hp):
    t = zr.shape[0]
    tm = _tile(t, 512)
    tab_nblk = cos.shape[0] // tm
    nq = N_HEADS * HEAD_PAD
    in_specs = [pl.BlockSpec((tm, ZR_QA_BLK[0]), lambda i: (i, ZR_QA_BLK[1])),
                pl.BlockSpec((tm, ZR_KV_BLK[0]), lambda i: (i, ZR_KV_BLK[1])),
                pl.BlockSpec((tm, HEAD_PAD), lambda i: (i % tab_nblk, 0)),
                pl.BlockSpec((tm, HEAD_PAD), lambda i: (i % tab_nblk, 0)),
                _const_spec((1, Q_LORA)), _const_spec((1, KV_LORA)),
                _const_spec(lw["wq"][0].shape), _const_spec(lw["wq"][1].shape),
                _const_spec((1, nq)), _const_spec((1, nq)),
                _const_spec(lw["wkv"][0].shape), _const_spec(lw["wkv"][1].shape),
                _const_spec((1, nq))]
    args = [zr, zr, cos, sin, lw["g_qa"], lw["g_kva"], *lw["wq"], lw["g_qn"], qpost, *lw["wkv"], lw["g_kn"]]
    q_w, q_dt = (2 * nq, BF16) if want_kv else (nq, F32)
    out_specs = [pl.BlockSpec((tm, q_w), lambda i: (i, 0)),
                 pl.BlockSpec((tm, KV_LORA), lambda i: (i, 0)),
                 pl.BlockSpec((tm, HEAD_PAD), lambda i: (i, 0))]
    out_shape = [jax.ShapeDtypeStruct((t, q_w), q_dt),
                 jax.ShapeDtypeStruct((t, KV_LORA), F32),
                 jax.ShapeDtypeStruct((t, HEAD_PAD), F32)]
    if want_kv:
        out_specs += [pl.BlockSpec((tm, 2 * nq), lambda i: (i, 0)),
                      pl.BlockSpec((tm, nq), lambda i: (i, 0))]
        out_shape += [jax.ShapeDtypeStruct((t, 2 * nq), BF16),
                      jax.ShapeDtypeStruct((t, nq), BF16)]
    return pl.pallas_call(
        functools.partial(_mla_prep_kernel, want_kv=want_kv, hp=hp),
        grid=(t // tm,),
        in_specs=in_specs, out_specs=out_specs, out_shape=out_shape,
        compiler_params=_cparams("parallel"),
        name="mla_prep",
    )(*args)


def _mla_prompt_kernel(qi_ref, ki_ref, q_ref, k_ref, v_ref, o_ref, m_s, l_s, acc_s, *, tq, sub):
    t = pl.program_id(2)
    qi = qi_ref[t]
    ki = ki_ref[t]

    @pl.when(ki == 0)
    def _():
        m_s[...] = jnp.full(m_s.shape, -jnp.inf, F32)
        l_s[...] = jnp.zeros(l_s.shape, F32)
        acc_s[...] = jnp.zeros(acc_s.shape, F32)

    def step(diagonal):
        for j in range(2):
            base = 2 * j * HEAD_PAD
            for r in range(tq // sub):
                rows = slice(r * sub, (r + 1) * sub)
                n_k = (r + 1) * sub if diagonal else tq
                k_hi = k_ref[0:n_k, base:base + HEAD_PAD]
                k_lo = k_ref[0:n_k, base + HEAD_PAD:base + 2 * HEAD_PAD]
                vv = v_ref[0:n_k, :]
                q2 = q_ref[rows, base:base + 2 * HEAD_PAD]
                s = (lax.dot_general(q2, jnp.concatenate([k_hi, k_hi], axis=1), _NT,
                                     preferred_element_type=F32)
                     + lax.dot_general(q2[:, :HEAD_PAD], k_lo, _NT, preferred_element_type=F32))
                if diagonal:
                    row = r * sub + lax.broadcasted_iota(jnp.int32, (sub, n_k), 0)
                    col = lax.broadcasted_iota(jnp.int32, (sub, n_k), 1)
                    s = jnp.where(col <= row, s, -jnp.inf)
                m_prev = m_s[j, rows]
                m_new = jnp.maximum(m_prev, jnp.max(s, axis=-1, keepdims=True))
                alpha = jnp.exp(m_prev - m_new)
                p = jnp.exp(s - m_new)
                l_s[j, rows] = alpha * l_s[j, rows] + jnp.sum(p, axis=-1, keepdims=True)
                p_hi, p_lo = _split(p)
                pv = jnp.dot(p_hi, vv, preferred_element_type=F32)
                pv = pv[:, :HEAD_PAD] + (pv[:, HEAD_PAD:]
                                         + jnp.dot(p_lo, vv[:, :HEAD_PAD], preferred_element_type=F32))
                acc_s[j, rows] = alpha * acc_s[j, rows] + pv
                m_s[j, rows] = m_new

    @pl.when(ki < qi)
    def _():
        step(False)

    @pl.when(ki == qi)
    def _():
        step(True)
        lane = lax.broadcasted_iota(jnp.int32, (tq, 2 * V_HEAD), 1)
        o_ref[...] = jnp.where(lane < V_HEAD, acc_s[0] / l_s[0], acc_s[1] / l_s[1])


def _mla_prompt(q2, k2, v2, *, nseq, seqlen):
    tq = _tile(seqlen, 1024)
    sub = _tile(tq, 256)
    nq = seqlen // tq
    qi_np = np.array([i for i in range(nq) for _ in range(i + 1)], np.int32)
    ki_np = np.array([j for i in range(nq) for j in range(i + 1)], np.int32)
    n_tri = int(qi_np.shape[0])
    pair_w = 4 * HEAD_PAD
    grid_spec = pltpu.PrefetchScalarGridSpec(
        num_scalar_prefetch=2,
        grid=(nseq, N_HEADS // 2, n_tri),
        in_specs=[pl.BlockSpec((tq, pair_w), lambda b, hp_, t, qi, ki: (b * nq + qi[t], hp_)),
                  pl.BlockSpec((tq, pair_w), lambda b, hp_, t, qi, ki: (b * nq + ki[t], hp_)),
                  pl.BlockSpec((tq, 2 * HEAD_PAD), lambda b, hp_, t, qi, ki: (b * nq + ki[t], hp_))],
        out_specs=pl.BlockSpec((tq, 2 * V_HEAD), lambda b, hp_, t, qi, ki: (b * nq + qi[t], hp_)),
        scratch_shapes=[pltpu.VMEM((2, tq, 1), F32), pltpu.VMEM((2, tq, 1), F32),
                        pltpu.VMEM((2, tq, 2 * V_HEAD), F32)])
    return pl.pallas_call(
        functools.partial(_mla_prompt_kernel, tq=tq, sub=sub),
        grid_spec=grid_spec,
        out_shape=jax.ShapeDtypeStruct((nseq * seqlen, N_HEADS * V_HEAD), F32),
        compiler_params=_cparams("parallel", "parallel", "arbitrary"),
        name="mla_prompt",
    )(jnp.asarray(qi_np), jnp.asarray(ki_np), q2, k2, v2)


def _mla_sample_kernel(pt_ref, q_ref, ckv_hbm, kpe_hbm, cn_ref, kn_ref, wkt_ref, wktp_ref, fold_ref, wv_ref,
                       o_ref, lhs_s, qr_s, m_s, l_s, ctx_s, cbuf, kbuf, sem, *, layer, pps, cpp, n_new):
    b = pl.program_id(0)
    s_idx = pl.program_id(1)
    n_steps = pl.num_programs(1)
    n_rows = N_HEADS * n_new
    g = b * n_steps + s_idx
    slot = g % 2

    def page_copies(bb, ss, sl):
        cps = []
        for i in range(pps):
            page = pt_ref[bb, ss * pps + i]
            cps.append(pltpu.make_async_copy(ckv_hbm.at[layer, page],
                                             cbuf.at[sl, pl.ds(i * PAGE, PAGE), :], sem.at[0, sl]))
            cps.append(pltpu.make_async_copy(kpe_hbm.at[layer, page],
                                             kbuf.at[sl, :, pl.ds(i * PAGE, PAGE)], sem.at[1, sl]))
        return cps

    @pl.when(g == 0)
    def _():
        for cp in page_copies(b, s_idx, slot):
            cp.start()

    @pl.when(g + 1 < pl.num_programs(0) * n_steps)
    def _():
        for cp in page_copies((g + 1) // n_steps, (g + 1) % n_steps, 1 - slot):
            cp.start()

    for cp in page_copies(b, s_idx, slot):
        cp.wait()

    @pl.when(s_idx == 0)
    def _():
        q = q_ref[...]
        lane_head = lax.broadcasted_iota(jnp.int32, q.shape, 1) // HEAD_PAD
        qblk = jnp.concatenate([jnp.where(lane_head == h, q, 0.0) for h in range(N_HEADS)], axis=0)
        qblk = qblk.astype(BF16)
        lhs_s[0:N_HEADS * QK_NOPE, :] = wkt_ref[...]
        lhs_s[N_HEADS * QK_NOPE:, :] = jnp.dot(qblk, wktp_ref[...], preferred_element_type=F32).astype(BF16)
        qr_s[...] = jnp.dot(qblk, fold_ref[...], preferred_element_type=F32).astype(BF16)
        m_s[...] = jnp.full(m_s.shape, -jnp.inf, F32)
        l_s[...] = jnp.zeros(l_s.shape, F32)
        ctx_s[...] = jnp.zeros(ctx_s.shape, F32)

    def scores(c, sr, ssr):
        cb = c.astype(BF16)
        big = lax.dot_general(lhs_s[...], cb, _NT, preferred_element_type=F32)
        pieces = []
        for h in range(N_HEADS):
            kn = big[h * QK_NOPE:(h + 1) * QK_NOPE]
            ss = jnp.sum(kn * kn, axis=0, keepdims=True) + ssr
            rk = lax.rsqrt(ss * (1.0 / QK_DIM) + EPS)
            rows = slice(N_HEADS * QK_NOPE + h * n_new, N_HEADS * QK_NOPE + (h + 1) * n_new)
            pieces.append((big[rows] + sr[h * n_new:(h + 1) * n_new]) * rk)
        return cb, jnp.concatenate(pieces, axis=0)

    def update(chunks):
        m_new = m_s[...]
        for _, sc in chunks:
            m_new = jnp.maximum(m_new, jnp.max(sc, axis=-1, keepdims=True))
        alpha = jnp.exp(m_s[...] - m_new)
        l_new = alpha * l_s[...]
        ctx = alpha * ctx_s[...]
        for cb, sc in chunks:
            p = jnp.exp(sc - m_new)
            l_new = l_new + jnp.sum(p, axis=-1, keepdims=True)
            ctx = ctx + jnp.dot(p.astype(BF16), cb, preferred_element_type=F32)
        l_s[...] = l_new
        ctx_s[...] = ctx
        m_s[...] = m_new

    chunks = []
    for j in range(0, pps, cpp):
        keys = pl.ds(j * PAGE, cpp * PAGE)
        kpe_t = kbuf[slot, :, keys]
        chunks.append(scores(cbuf[slot, keys, :],
                             jnp.dot(qr_s[...], kpe_t.astype(BF16), preferred_element_type=F32),
                             jnp.sum(kpe_t * kpe_t, axis=0, keepdims=True)))
    update(chunks)

    @pl.when(s_idx == pl.num_programs(1) - 1)
    def _():
        cpad = jnp.concatenate([cn_ref[...], jnp.zeros((PAGE - n_new, KV_LORA), F32)], axis=0)
        kpad = jnp.concatenate([kn_ref[...], jnp.zeros((PAGE - n_new, QK_ROPE), F32)], axis=0)
        key = lax.broadcasted_iota(jnp.int32, (n_rows, PAGE), 1)
        qry = lax.broadcasted_iota(jnp.int32, (n_rows, PAGE), 0) % n_new
        sr = lax.dot_general(qr_s[...], kpad.astype(BF16), _NT, preferred_element_type=F32)
        ones = jnp.ones((8, QK_ROPE), BF16)
        ssr = lax.dot_general(ones, (kpad * kpad).astype(BF16), _NT, preferred_element_type=F32)[0:1]
        cb, sc = scores(cpad, sr, ssr)
        update([(cb, jnp.where(key <= qry, sc, -jnp.inf))])
        ctx = (ctx_s[...] / l_s[...]).astype(BF16)
        res = jnp.dot(ctx, wv_ref[...], preferred_element_type=F32)
        lane_head = lax.broadcasted_iota(jnp.int32, (n_new, N_HEADS * V_HEAD), 1) // V_HEAD
        out = jnp.zeros((n_new, N_HEADS * V_HEAD), F32)
        for h in range(N_HEADS):
            out = out + jnp.where(lane_head == h, res[h * n_new:(h + 1) * n_new], 0.0)
        o_ref[...] = out


def _mla_sample(q, c_new, kpe_new, cache_ckv, cache_kpe_t, page_table, layer, lw):
    nseq, n_new, _ = q.shape
    n_pages = page_table.shape[1]
    pps = _tile(n_pages, 16)
    n_steps = n_pages // pps
    n_rows = N_HEADS * n_new

    in_specs = [pl.BlockSpec((None, n_new, N_HEADS * HEAD_PAD), lambda b, s, pt: (b, 0, 0)),
                pl.BlockSpec(memory_space=pl.ANY), pl.BlockSpec(memory_space=pl.ANY),
                pl.BlockSpec((None, n_new, KV_LORA), lambda b, s, pt: (b, 0, 0)),
                pl.BlockSpec((None, n_new, QK_ROPE), lambda b, s, pt: (b, 0, 0)),
                _const_spec(lw["wkt"].shape), _const_spec(lw["wktp"].shape),
                _const_spec(lw["fold"].shape), _const_spec(lw["wv"].shape)]
    grid_spec = pltpu.PrefetchScalarGridSpec(
        num_scalar_prefetch=1,
        grid=(nseq, n_steps),
        in_specs=in_specs,
        out_specs=pl.BlockSpec((None, n_new, N_HEADS * V_HEAD), lambda b, s, pt: (b, 0, 0)),
        scratch_shapes=[pltpu.VMEM((N_HEADS * QK_NOPE + n_rows, KV_LORA), BF16),
                        pltpu.VMEM((n_rows, QK_ROPE), BF16),
                        pltpu.VMEM((n_rows, 1), F32), pltpu.VMEM((n_rows, 1), F32),
                        pltpu.VMEM((n_rows, KV_LORA), F32),
                        pltpu.VMEM((2, pps * PAGE, KV_LORA), F32),
                        pltpu.VMEM((2, QK_ROPE, pps * PAGE), F32),
                        pltpu.SemaphoreType.DMA((2, 2))])
    return pl.pallas_call(
        functools.partial(_mla_sample_kernel, layer=layer, pps=pps, cpp=_tile(pps, 4), n_new=n_new),
        grid_spec=grid_spec,
        out_shape=jax.ShapeDtypeStruct((nseq, n_new, N_HEADS * V_HEAD), F32),
        compiler_params=_cparams("arbitrary", "arbitrary"),
        name="mla_sample",
    )(page_table, q, cache_ckv, cache_kpe_t, c_new, kpe_new,
      lw["wkt"], lw["wktp"], lw["fold"], lw["wv"])


def _memkv_kernel(x_ref, g_ref, wh_ref, wl_ref, gk_ref, k_ref, v_ref, *, hp):
    kv = _mm(_rms(x_ref[...], g_ref[...]), wh_ref[...], wl_ref[...], hp)
    for h in range(MEM_HEADS):
        sl = slice(h * MEM_HD, (h + 1) * MEM_HD)
        k_ref[:, sl] = _rms(kv[:, sl], gk_ref[...])
    v_ref[...] = kv[:, MEM_W:]


def _memkv(mem, g_mem, wmkv_hl, g_mkn, *, hp):
    t = mem.shape[0]
    tm = _tile(t, 256)
    return pl.pallas_call(
        functools.partial(_memkv_kernel, hp=hp),
        grid=(t // tm,),
        in_specs=[pl.BlockSpec((tm, D_MODEL), lambda i: (i, 0)),
                  _const_spec((1, D_MODEL)), _const_spec(wmkv_hl[0].shape), _const_spec(wmkv_hl[1].shape),
                  _const_spec((1, MEM_HD))],
        out_specs=[pl.BlockSpec((tm, MEM_W), lambda i: (i, 0))] * 2,
        out_shape=[jax.ShapeDtypeStruct((t, MEM_W), F32)] * 2,
        compiler_params=_cparams("parallel"),
        name="memkv",
    )(mem, g_mem, *wmkv_hl, g_mkn)


def _memattn_kernel(q_ref, k_ref, v_ref, g_ref, o_ref, *, hp, rows_by_head):
    scale = 1.0 / math.sqrt(MEM_HD)
    for h in range(MEM_HEADS):
        sl = slice(h * MEM_HD, (h + 1) * MEM_HD)
        if rows_by_head:
            k = k_ref[pl.ds(h, N_MEM, stride=MEM_HEADS), :]
            v = v_ref[pl.ds(h, N_MEM, stride=MEM_HEADS), :]
        else:
            k = k_ref[:, sl]
            v = v_ref[:, sl]
        q = _rms(q_ref[:, sl], g_ref[...]) * scale
        s = _mm2(q, k, hp, _NT)
        p = jnp.exp(s - jnp.max(s, axis=-1, keepdims=True))
        l = jnp.sum(p, axis=-1, keepdims=True)
        o_ref[:, sl] = _mm2(p, v, hp) / l


def _memattn(zr, mk, mv, g_mqn, *, nseq, seqlen, kv_index, hp):
    tq = _tile(seqlen, 512)
    nq = seqlen // tq
    lead = (None,) * (mk.ndim - 2)
    kv_blk = (N_MEM, MEM_W) if mk.shape[-1] == MEM_W else (N_MEM * MEM_HEADS, MEM_HD)
    return pl.pallas_call(
        functools.partial(_memattn_kernel, hp=hp, rows_by_head=kv_blk[1] == MEM_HD),
        grid=(nseq, nq),
        in_specs=[pl.BlockSpec((tq, ZR_MQ_BLK[0]), lambda b, i: (b * nq + i, ZR_MQ_BLK[1])),
                  pl.BlockSpec(lead + kv_blk, lambda b, i: kv_index(b)),
                  pl.BlockSpec(lead + kv_blk, lambda b, i: kv_index(b)),
                  _const_spec((1, MEM_HD))],
        out_specs=pl.BlockSpec((tq, MEM_W), lambda b, i: (b * nq + i, 0)),
        out_shape=jax.ShapeDtypeStruct((nseq * seqlen, MEM_W), F32),
        compiler_params=_cparams("parallel", "parallel"),
        name="memattn",
    )(zr, mk, mv, g_mqn)


def _merge_kernel(x_ref, ya_ref, yb_ref, yc_ref, ym_ref, zg_ref, wbrh_ref, wbrl_ref, woh_ref, wol_ref,
                  gf_ref, wrh_ref, wrl_ref, br_ref, x1_ref, h2_ref, cw_ref, *, hp):
    acc = None
    for b, y_ref in enumerate((ya_ref, yb_ref, yc_ref, ym_ref)):
        term = zg_ref[:, b * D_MODEL:(b + 1) * D_MODEL] * _mm(y_ref[...], wbrh_ref[b], wbrl_ref[b], hp)
        acc = term if acc is None else acc + term
    x1 = x_ref[...] + _mm(acc, woh_ref[...], wol_ref[...], hp)
    x1_ref[...] = x1
    h2 = _rms(x1, gf_ref[...])
    h2_ref[...] = h2
    logits = _mm(h2, wrh_ref[...], wrl_ref[...], True) + br_ref[...]
    lane = lax.broadcasted_iota(jnp.int32, logits.shape, 1).astype(F32)
    big = 1e9
    is_grp = (lane >= N_EXPERTS) & (lane < N_EXPERTS + N_GROUPS)
    gl = jnp.where(is_grp, logits, -jnp.inf)
    ge = jnp.exp(gl - jnp.max(gl, axis=-1, keepdims=True))
    gprob = ge / jnp.sum(ge, axis=-1, keepdims=True)
    gp = jnp.max(gprob, axis=-1, keepdims=True)
    gi = jnp.min(jnp.where(is_grp & (gprob == gp), lane, big), axis=-1, keepdims=True) - N_EXPERTS
    sel = (lane >= gi * EXP_PER_GROUP) & (lane < (gi + 1) * EXP_PER_GROUP)
    el = jnp.where(sel, logits, -jnp.inf)
    ee = jnp.exp(el - jnp.max(el, axis=-1, keepdims=True))
    eprob = ee / jnp.sum(ee, axis=-1, keepdims=True)
    v1 = jnp.max(eprob, axis=-1, keepdims=True)
    i1 = jnp.min(jnp.where(sel & (eprob == v1), lane, big), axis=-1, keepdims=True)
    rest = jnp.where(sel & (lane != i1), eprob, -1.0)
    v2 = jnp.max(rest, axis=-1, keepdims=True)
    i2 = jnp.min(jnp.where(rest == v2, lane, big), axis=-1, keepdims=True)
    den = v1 + v2
    cw_ref[...] = jnp.where(lane == i1, gp * v1 / den, jnp.where(lane == i2, gp * v2 / den, 0.0))


def _merge(x, ya, yb, yc, ym, zg, lw, *, hp):
    t = x.shape[0]
    tm = _tile(t, 256)
    row = lambda w: pl.BlockSpec((tm, w), lambda i: (i, 0))
    consts = [*lw["wbr"], *lw["wo"], lw["g_ffn"], *lw["wr"], lw["b_r"]]
    return pl.pallas_call(
        functools.partial(_merge_kernel, hp=hp),
        grid=(t // tm,),
        in_specs=[row(D_MODEL), row(POOL_W), row(512), row(SG_W), row(MEM_W), row(N_BRANCH * D_MODEL)]
                 + [_const_spec(c.shape) for c in consts],
        out_specs=[row(D_MODEL), row(D_MODEL), row(128)],
        out_shape=[jax.ShapeDtypeStruct((t, D_MODEL), F32),
                   jax.ShapeDtypeStruct((t, D_MODEL), F32),
                   jax.ShapeDtypeStruct((t, 128), F32)],
        compiler_params=_cparams("parallel"),
        name="merge",
    )(x, ya, yb, yc, ym, zg, *consts)


def _moe_kernel(h_ref, cw_ref, x_ref, wguh_ref, wgul_ref, wdh_ref, wdl_ref, o_ref, acc_s, hh_s, hl_s, *, hp):
    e = pl.program_id(1)

    @pl.when(e == 0)
    def _():
        acc_s[...] = jnp.zeros(acc_s.shape, F32)
        hh_s[...], hl_s[...] = _split(h_ref[...])

    dot = lambda a, b: jnp.dot(a, b, preferred_element_type=F32)
    ab = dot(hh_s[...], wguh_ref[...])
    if hp:
        ab = ab + (dot(hl_s[...], wguh_ref[...]) + dot(hh_s[...], wgul_ref[...]))
    cw = cw_ref[...]
    lane = lax.broadcasted_iota(jnp.int32, cw.shape, 1)
    cwe = jnp.sum(jnp.where(lane == e, cw, 0.0), axis=-1, keepdims=True)
    t = jax.nn.silu(ab[:, :D_EXPERT]) * ab[:, D_EXPERT:] * cwe
    acc_s[...] += _mm(t, wdh_ref[...], wdl_ref[...], hp)

    @pl.when(e == N_EXPERTS - 1)
    def _():
        o_ref[...] = x_ref[...] + acc_s[...]


def _moe(h2, cw, x1, wgu_hl, wd_hl, *, hp):
    t = h2.shape[0]
    tm = _tile(t, 512)
    wgu_spec = pl.BlockSpec((None, D_MODEL, 2 * D_EXPERT), lambda i, e: (e, 0, 0))
    wd_spec = pl.BlockSpec((None, D_EXPERT, D_MODEL), lambda i, e: (e, 0, 0))
    return pl.pallas_call(
        functools.partial(_moe_kernel, hp=hp),
        grid=(t // tm, N_EXPERTS),
        in_specs=[pl.BlockSpec((tm, D_MODEL), lambda i, e: (i, 0)),
                  pl.BlockSpec((tm, 128), lambda i, e: (i, 0)),
                  pl.BlockSpec((tm, D_MODEL), lambda i, e: (i, 0)),
                  wgu_spec, wgu_spec, wd_spec, wd_spec],
        out_specs=pl.BlockSpec((tm, D_MODEL), lambda i, e: (i, 0)),
        out_shape=jax.ShapeDtypeStruct((t, D_MODEL), F32),
        scratch_shapes=[pltpu.VMEM((tm, D_MODEL), F32), pltpu.VMEM((tm, D_MODEL), BF16),
                        pltpu.VMEM((tm, D_MODEL), BF16)],
        compiler_params=_cparams("parallel", "arbitrary"),
        name="moe",
    )(h2, cw, x1, *wgu_hl, *wd_hl)


def _hl(w):
    b = lax.bitcast_convert_type(w, jnp.int32)
    b = (b + jnp.int32(0x7FFF) + ((b >> 16) & 1)) & jnp.int32(-65536)
    hi = lax.bitcast_convert_type(b, F32)
    return hi.astype(BF16), (w - hi).astype(BF16)


def _rope_tables(pos):
    half = QK_ROPE // 2
    inv = ROPE_THETA ** (-jnp.arange(half, dtype=F32) / half)
    ang = pos.astype(F32)[:, None] * inv
    cos, sin = jnp.cos(ang), jnp.sin(ang)
    n = pos.shape[0]
    cos128 = jnp.concatenate([jnp.ones((n, QK_NOPE), F32), cos, cos, jnp.zeros((n, 32), F32)], 1)
    sin128 = jnp.concatenate([jnp.zeros((n, QK_NOPE), F32), sin, sin, jnp.zeros((n, 32), F32)], 1)
    return cos128, sin128


def _fold_matrix():
    f = np.zeros((N_HEADS * HEAD_PAD, QK_ROPE), np.float32)
    for h in range(N_HEADS):
        for j in range(QK_ROPE):
            f[h * HEAD_PAD + QK_NOPE + j, j] = 1.0
    return jnp.asarray(f, BF16)


def _head_pad_gain(g):
    return jnp.tile(jnp.concatenate([g, jnp.zeros((HEAD_PAD - QK_DIM,), F32)]), N_HEADS)[None]


def _layer_weights(l, w):
    half = QK_ROPE // 2
    wi = w["w_in"][l]
    w_a, w_qa = wi[:, 0:512], wi[:, 512:768]
    w_lat, w_rope = wi[:, 768:896], wi[:, 896:928]
    w_c, w_mq, w_g = wi[:, 928:1952], wi[:, 1952:2464], wi[:, 2464:]
    z64 = jnp.zeros((D_MODEL, QK_NOPE), F32)
    z32 = jnp.zeros((D_MODEL, 32), F32)
    rope128 = jnp.concatenate([z64, w_rope, z32], 1)
    ropesw128 = jnp.concatenate([z64, -w_rope[:, half:], w_rope[:, :half], z32], 1)
    lw = {}
    lw["wr_in"] = _hl(jnp.concatenate([w_c, w_a, w_mq, w_qa, w_lat, rope128, ropesw128], 1))
    lw["wg_in"] = _hl(w_g)
    lw["g_mix"] = w["g_mix"][l][None]

    wq = w["w_qb"][l].reshape(Q_LORA, N_HEADS, QK_DIM)
    nope, x1, x2 = wq[..., :QK_NOPE], wq[..., QK_NOPE:QK_NOPE + half], wq[..., QK_NOPE + half:]
    zq = lambda n: jnp.zeros((Q_LORA, N_HEADS, n), F32)
    wq1 = jnp.concatenate([nope, x1, x2, zq(32)], -1).reshape(Q_LORA, N_HEADS * HEAD_PAD)
    wq2 = jnp.concatenate([zq(QK_NOPE), -x2, x1, zq(32)], -1).reshape(Q_LORA, N_HEADS * HEAD_PAD)
    lw["wq"] = _hl(jnp.concatenate([wq1, wq2], 1))
    lw["g_qa"] = w["g_qa"][l][None]
    lw["g_kva"] = w["g_kva"][l][None]
    lw["g_qn"] = _head_pad_gain(w["g_qn"][l])
    lw["g_kn"] = _head_pad_gain(w["g_kn"][l])

    wkv = w["w_kvb"][l].reshape(KV_LORA, N_HEADS, QK_NOPE + V_HEAD)
    kn, vv = wkv[..., :QK_NOPE], wkv[..., QK_NOPE:]
    wk_pad = jnp.concatenate([kn, jnp.zeros((KV_LORA, N_HEADS, HEAD_PAD - QK_NOPE), F32)], -1)
    wk_pad = wk_pad.reshape(KV_LORA, N_HEADS * HEAD_PAD)
    wv = vv.reshape(KV_LORA, N_HEADS * V_HEAD)
    lw["wkv"] = _hl(jnp.concatenate([wk_pad, wv], 1))
    lw["wkt"] = kn.reshape(KV_LORA, N_HEADS * QK_NOPE).T.astype(BF16)
    lw["wktp"] = wk_pad.T.astype(BF16)
    lw["wv"] = wv.astype(BF16)
    lw["fold"] = _fold_matrix()

    lw["w_pool"] = _hl(w["w_pool"][l])
    lw["pool_scale"] = w["pool_scale"][l][None]
    lw["g_sgv"] = w["g_sgv"][l]
    lw["w_sp"] = w["w_sp"][l]
    lw["b_sp"] = w["b_sp"][l]
    lw["g_mem"] = w["g_mem"][l][None]
    lw["wmkv"] = _hl(jnp.concatenate([w["w_mk"][l], w["w_mv"][l]], 1))
    lw["g_mqn"] = w["g_mqn"][l][None]
    lw["g_mkn"] = w["g_mkn"][l][None]
    lw["wbr"] = _hl(w["w_br"][l])
    lw["wo"] = _hl(w["w_o"][l])
    lw["g_ffn"] = w["g_ffn"][l][None]
    lw["wr"] = _hl(jnp.concatenate([w["w_re"][l], w["w_rg"][l],
                                    jnp.zeros((D_MODEL, 128 - N_EXPERTS - N_GROUPS), F32)], 1))
    lw["b_r"] = jnp.concatenate([w["b_re"][l], w["b_rg"][l],
                                 jnp.zeros((128 - N_EXPERTS - N_GROUPS,), F32)])[None]
    lw["wgu"] = _hl(jnp.concatenate([w["w_eg"][l], w["w_eu"][l]], -1))
    lw["wd"] = _hl(w["w_ed"][l])
    return lw


def _sgu_operands(lw, seqlen):
    blk = min(seqlen, CHUNK)
    rep = CHUNK // blk
    ws = jnp.tile(lw["w_sp"][:, :blk, :blk], (1, rep, rep))
    b = jnp.tile(lw["b_sp"][:, :blk], (1, rep))
    bfull = jnp.repeat(b.T, CHUNK, axis=1)
    return ws, bfull, blk


def _group_layer(x, lw, *, nseq, seqlen, pos0, cos, sin, pool_buf, mem_k, mem_v, kv_index,
                 mla_sample_fn, hp):
    zr = _inproj(x, lw["g_mix"], lw["wr_in"], hp=hp, sigmoid=False)
    zg = _inproj(x, lw["g_mix"], lw["wg_in"], hp=hp, sigmoid=True)
    ya, pool_new = _pool(zr, pool_buf, lw["w_pool"], lw["pool_scale"],
                         nseq=nseq, seqlen=seqlen, pos0=pos0, hp=hp)
    ws, bfull, blk = _sgu_operands(lw, seqlen)
    yc, v_rows = _sgu(zr, lw["g_sgv"], ws, bfull, blk=blk, hp=hp)
    scale = 1.0 / math.sqrt(QK_DIM)
    if mla_sample_fn is not None:
        q, c, kpe128 = _mla_prep(zr, cos, sin, lw, lw["g_kn"] * scale, want_kv=False, hp=hp)
        kpe = kpe128[:, QK_NOPE:QK_DIM]
        yb = mla_sample_fn(q, c, kpe)
    else:
        qpost = jnp.full((1, N_HEADS * HEAD_PAD), scale, F32)
        q, c, kpe128, k, v = _mla_prep(zr, cos, sin, lw, qpost, want_kv=True, hp=hp)
        kpe = kpe128[:, QK_NOPE:QK_DIM]
        yb = _mla_prompt(q, k, v, nseq=nseq, seqlen=seqlen)
    ym = _memattn(zr, mem_k, mem_v, lw["g_mqn"], nseq=nseq, seqlen=seqlen, kv_index=kv_index, hp=hp)
    x1, h2, cw = _merge(x, ya, yb, yc, ym, zg, lw, hp=hp)
    x2 = _moe(h2, cw, x1, lw["wgu"], lw["wd"], hp=hp)
    return x2, c, kpe, pool_new, v_rows


def kernel(x_prompt, x_sample, mem_prompt, cache_ckv, cache_kpe, cache_memk, cache_memv, state_pool, page_table, g_mix, w_in, g_qa, w_qb, g_kva, w_kvb, g_qn, g_kn, w_pool, pool_scale, g_sgv, w_sp, b_sp, g_mem, w_mk, w_mv, g_mqn, g_mkn, w_br, w_o, g_ffn, w_rg, b_rg, w_re, b_re, w_eg, w_eu, w_ed):
    w = dict(g_mix=g_mix, w_in=w_in, g_qa=g_qa, w_qb=w_qb, g_kva=g_kva, w_kvb=w_kvb, g_qn=g_qn, g_kn=g_kn,
             w_pool=w_pool, pool_scale=pool_scale, g_sgv=g_sgv, w_sp=w_sp, b_sp=b_sp, g_mem=g_mem,
             w_mk=w_mk, w_mv=w_mv, g_mqn=g_mqn, g_mkn=g_mkn, w_br=w_br, w_o=w_o, g_ffn=g_ffn,
             w_rg=w_rg, b_rg=b_rg, w_re=w_re, b_re=b_re, w_eg=w_eg, w_eu=w_eu, w_ed=w_ed)
    nb, seq, _ = x_prompt.shape
    ndb, t_new, _ = x_sample.shape
    depth = w_in.shape[0]
    past = page_table.shape[1] * PAGE

    cos_p, sin_p = _rope_tables(jnp.arange(seq, dtype=jnp.int32))
    cos_s, sin_s = _rope_tables(past + jnp.arange(t_new, dtype=jnp.int32))
    tm_s = _tile(ndb * t_new, 512)
    cos_s = jnp.tile(cos_s, (tm_s // t_new, 1))
    sin_s = jnp.tile(sin_s, (tm_s // t_new, 1))

    xp = x_prompt.reshape(nb * seq, D_MODEL)
    xs = x_sample.reshape(ndb * t_new, D_MODEL)
    mem = mem_prompt.reshape(nb * N_MEM, D_MODEL)
    zero_buf = jnp.zeros((nb, POOL_PAD, POOL_W), F32)
    memk_s = cache_memk.reshape(depth, ndb, N_MEM * MEM_HEADS, MEM_HD)
    memv_s = cache_memv.reshape(depth, ndb, N_MEM * MEM_HEADS, MEM_HD)
    cache_kpe_t = jnp.swapaxes(cache_kpe, 2, 3)

    outs = [[] for _ in range(9)]
    for l in range(depth):
        lw = _layer_weights(l, w)
        mk, mv = _memkv(mem, lw["g_mem"], lw["wmkv"], lw["g_mkn"], hp=True)
        xp, c, kp, pb, _ = _group_layer(
            xp, lw, nseq=nb, seqlen=seq, pos0=0, cos=cos_p, sin=sin_p, pool_buf=zero_buf,
            mem_k=mk, mem_v=mv, kv_index=lambda b: (b, 0), mla_sample_fn=None, hp=True)
        outs[0].append(c.reshape(nb, seq, KV_LORA))
        outs[1].append(kp.reshape(nb, seq, QK_ROPE))
        outs[2].append(mk.reshape(nb, N_MEM, MEM_HEADS, MEM_HD))
        outs[3].append(mv.reshape(nb, N_MEM, MEM_HEADS, MEM_HD))
        outs[4].append(pb[:, 1:])

        def mla_sample_fn(q, c_new, kpe_new, l=l, lw=lw):
            o = _mla_sample(q.reshape(ndb, t_new, -1), c_new.reshape(ndb, t_new, KV_LORA),
                            kpe_new.reshape(ndb, t_new, QK_ROPE), cache_ckv, cache_kpe_t, page_table, l, lw)
            return o.reshape(ndb * t_new, N_HEADS * V_HEAD)

        buf_s = jnp.pad(state_pool[l], ((0, 0), (1, 0), (0, 0)))
        xs, c, kp, pb, vr = _group_layer(
            xs, lw, nseq=ndb, seqlen=t_new, pos0=past, cos=cos_s, sin=sin_s, pool_buf=buf_s,
            mem_k=memk_s, mem_v=memv_s, kv_index=lambda b, l=l: (l, b, 0, 0),
            mla_sample_fn=mla_sample_fn, hp=False)
        outs[5].append(c.reshape(ndb, t_new, KV_LORA))
        outs[6].append(kp.reshape(ndb, t_new, QK_ROPE))
        outs[7].append(pb[:, 1:])
        outs[8].append(vr.reshape(ndb, t_new, SG_W))

    stacked = [jnp.stack(o) for o in outs]
    return (xp.reshape(nb, seq, D_MODEL), xs.reshape(ndb, t_new, D_MODEL), *stacked)
```

```python
import functools
import math

import numpy as np
import jax
import jax.numpy as jnp
from jax import lax
from jax.experimental import pallas as pl
from jax.experimental.pallas import tpu as pltpu

F32 = jnp.float32
BF16 = jnp.bfloat16
EPS = 1e-6

D_MODEL = 1024
PAGE = 128
N_MEM = 256
POOL_WINDOWS = (2, 4, 8, 16)
POOL_GW = 128
POOL_W = 512
POOL_BUF = 15
POOL_PAD = 16
N_HEADS = 8
QK_NOPE = 64
QK_ROPE = 32
QK_DIM = QK_NOPE + QK_ROPE
V_HEAD = 64
HEAD_PAD = 128
Q_LORA = 256
KV_LORA = 128
ROPE_THETA = 10000.0
CHUNK = 128
SG_GROUPS = 4
SG_W = 512
MEM_HEADS = 4
MEM_HD = 128
MEM_W = 512
N_BRANCH = 4
N_GROUPS = 4
EXP_PER_GROUP = 4
N_EXPERTS = 16
D_EXPERT = 256
GROUP_LANE = N_EXPERTS

ZR_W = 2688
ZR_C_BLK = (1024, 0)
ZR_A_BLK = (512, 2)
ZR_MQ_BLK = (512, 3)
ZR_QA_BLK = (256, 8)
ZR_KV_BLK = (384, 6)

VMEM_LIMIT = 56 * 1024 * 1024

_NN = (((1,), (0,)), ((), ()))
_NT = (((1,), (1,)), ((), ()))


def _tile(n, pref):
    return pref if n % pref == 0 else n


def _cparams(*sem):
    return pltpu.CompilerParams(dimension_semantics=sem, vmem_limit_bytes=VMEM_LIMIT)


def _rms(x, g):
    return x * lax.rsqrt(jnp.mean(x * x, axis=-1, keepdims=True) + EPS) * g


def _const_spec(shape):
    nd = len(shape)
    return pl.BlockSpec(shape, lambda *_: (0,) * nd, pipeline_mode=pl.Buffered(1))


def _split(a):
    hi = lax.bitcast_convert_type(lax.bitcast_convert_type(a, jnp.int32) & jnp.int32(-65536), F32)
    return hi.astype(BF16), (a - hi).astype(BF16)


def _mm(a, w_hi, w_lo, hp, dims=_NN):
    dot = lambda x, y: lax.dot_general(x, y, dims, preferred_element_type=F32)
    if not hp:
        return dot(a.astype(BF16), w_hi)
    a_hi, a_lo = _split(a)
    return dot(a_hi, w_hi) + (dot(a_lo, w_hi) + dot(a_hi, w_lo))


def _mm2(a, b, hp, dims=_NN):
    if not hp:
        return lax.dot_general(a.astype(BF16), b.astype(BF16), dims, preferred_element_type=F32)
    b_hi, b_lo = _split(b)
    return _mm(a, b_hi, b_lo, True, dims)


def _inproj_kernel(x_ref, g_ref, wh_ref, wl_ref, z_ref, *, hp, sigmoid):
    h = _rms(x_ref[...], g_ref[...])
    z = _mm(h, wh_ref[...], wl_ref[...], hp)
    z_ref[...] = jax.nn.sigmoid(z) if sigmoid else z


def _inproj(x, g, w_hl, *, hp, sigmoid):
    t = x.shape[0]
    tm = _tile(t, 256)
    n = w_hl[0].shape[1]
    return pl.pallas_call(
        functools.partial(_inproj_kernel, hp=hp, sigmoid=sigmoid),
        grid=(t // tm,),
        in_specs=[pl.BlockSpec((tm, D_MODEL), lambda i: (i, 0)),
                  _const_spec((1, D_MODEL)),
                  _const_spec(w_hl[0].shape), _const_spec(w_hl[1].shape)],
        out_specs=pl.BlockSpec((tm, n), lambda i: (i, 0)),
        out_shape=jax.ShapeDtypeStruct((t, n), F32),
        compiler_params=_cparams("parallel"),
        name="inproj",
    )(x, g, *w_hl)


def _pool_kernel(a_ref, buf_ref, wph_ref, wpl_ref, ps_ref, y_ref, pn_ref, ext_ref, *, nb, ts, pos0, hp):
    s = pl.program_id(1)

    @pl.when(s == 0)
    def _():
        ext_ref[:, 0:POOL_PAD, :] = buf_ref[...]

    a = a_ref[...].reshape(nb, ts, POOL_W)
    ext_ref[:, POOL_PAD:POOL_PAD + ts, :] = a
    pos = pos0 + s * ts + lax.broadcasted_iota(jnp.int32, (1, ts, 1), 1)
    for g, w in enumerate(POOL_WINDOWS):
        sl = slice(g * POOL_GW, (g + 1) * POOL_GW)
        acc = a[:, :, sl]
        for k in range(1, w):
            acc = acc + ext_ref[:, POOL_PAD - k:POOL_PAD - k + ts, sl]
        cnt = jnp.minimum(pos + 1, w).astype(F32)
        m = (acc / cnt - a[:, :, sl]).reshape(nb * ts, POOL_GW)
        y_ref[:, sl] = _mm(m, wph_ref[g], wpl_ref[g], hp) * ps_ref[:, sl]
    tail = ext_ref[:, ts:ts + POOL_PAD, :]
    pn_ref[...] = tail
    ext_ref[:, 0:POOL_PAD, :] = tail


def _pool(zr, buf, wp_hl, ps, *, nseq, seqlen, pos0, hp):
    if seqlen >= 512:
        nb, ts = 1, 512
    else:
        nb, ts = _tile(nseq, 16), seqlen
    n_s = seqlen // ts
    kern = functools.partial(_pool_kernel, nb=nb, ts=ts, pos0=pos0, hp=hp)
    return pl.pallas_call(
        kern,
        grid=(nseq // nb, n_s),
        in_specs=[pl.BlockSpec((nb * ts, POOL_W), lambda b, s: (b * n_s + s, ZR_A_BLK[1])),
                  pl.BlockSpec((nb, POOL_PAD, POOL_W), lambda b, s: (b, 0, 0)),
                  _const_spec(wp_hl[0].shape), _const_spec(wp_hl[1].shape),
                  _const_spec((1, POOL_W))],
        out_specs=[pl.BlockSpec((nb * ts, POOL_W), lambda b, s: (b * n_s + s, 0)),
                   pl.BlockSpec((nb, POOL_PAD, POOL_W), lambda b, s: (b, 0, 0))],
        out_shape=[jax.ShapeDtypeStruct((nseq * seqlen, POOL_W), F32),
                   jax.ShapeDtypeStruct((nseq, POOL_PAD, POOL_W), F32)],
        scratch_shapes=[pltpu.VMEM((nb, POOL_PAD + ts, POOL_W), F32)],
        compiler_params=_cparams("parallel", "arbitrary"),
        name="pool",
    )(zr, buf, *wp_hl, ps)


def _sgu_kernel(zc_ref, g_ref, ws_ref, b_ref, y_ref, v_ref, *, n_chunk, blk, hp):
    z = jax.nn.gelu(zc_ref[...])
    r = lax.broadcasted_iota(jnp.int32, (CHUNK, CHUNK), 0)
    c = lax.broadcasted_iota(jnp.int32, (CHUNK, CHUNK), 1)
    mask = (c <= r) & ((r // blk) == (c // blk))
    for g in range(SG_GROUPS):
        sl = slice(g * CHUNK, (g + 1) * CHUNK)
        vg = _rms(z[:, SG_W + g * CHUNK:SG_W + (g + 1) * CHUNK], g_ref[g:g + 1, :])
        v_ref[:, sl] = vg
        ws = jnp.where(mask, ws_ref[g], 0.0)
        for ch in range(n_chunk):
            rows = slice(ch * CHUNK, (ch + 1) * CHUNK)
            sp = _mm2(ws, vg[rows], hp) + b_ref[:, sl]
            y_ref[rows, sl] = z[rows, sl] * sp


def _sgu(zr, g_sgv, ws, bfull, *, blk, hp):
    t = zr.shape[0]
    rows = _tile(t, 512)
    n_chunk = rows // CHUNK
    return pl.pallas_call(
        functools.partial(_sgu_kernel, n_chunk=n_chunk, blk=blk, hp=hp),
        grid=(t // rows,),
        in_specs=[pl.BlockSpec((rows, ZR_C_BLK[0]), lambda i: (i, ZR_C_BLK[1])),
                  _const_spec(g_sgv.shape),
                  _const_spec(ws.shape),
                  _const_spec(bfull.shape)],
        out_specs=[pl.BlockSpec((rows, SG_W), lambda i: (i, 0))] * 2,
        out_shape=[jax.ShapeDtypeStruct((t, SG_W), F32)] * 2,
        compiler_params=_cparams("parallel"),
        name="sgu",
    )(zr, g_sgv, ws, bfull)


def _mla_prep_kernel(qa_ref, kv_ref, cos_ref, sin_ref, gqa_ref, gkva_ref, wqh_ref, wql_ref, gqn_ref,
                     qpost_ref, wkvh_ref, wkvl_ref, gkn_ref, q_ref, c_ref, kpe_ref, *maybe_kv,
                     want_kv, hp):
    cos = cos_ref[...]
    sin = sin_ref[...]
    qn = _rms(qa_ref[...], gqa_ref[...])
    qq = _mm(qn, wqh_ref[...], wql_ref[...], hp)
    kv = kv_ref[...]
    c = _rms(kv[:, :KV_LORA], gkva_ref[...])
    c_ref[...] = c
    kpe = kv[:, KV_LORA:2 * KV_LORA] * cos + kv[:, 2 * KV_LORA:] * sin
    kpe_ref[...] = kpe
    nq = N_HEADS * HEAD_PAD

    def put_split(ref, idx, val):
        hi, lo = _split(val)
        ref[:, 2 * idx * HEAD_PAD:(2 * idx + 1) * HEAD_PAD] = hi
        ref[:, (2 * idx + 1) * HEAD_PAD:(2 * idx + 2) * HEAD_PAD] = lo

    if want_kv:
        k_ref, v_ref = maybe_kv
        kvx = _mm(c, wkvh_ref[...], wkvl_ref[...], hp)
        for pair in range(N_HEADS // 2):
            put_split(v_ref, pair, kvx[:, nq + pair * HEAD_PAD:nq + (pair + 1) * HEAD_PAD])
    for h in range(N_HEADS):
        sl = slice(h * HEAD_PAD, (h + 1) * HEAD_PAD)
        qh = qq[:, sl] * cos + qq[:, nq + h * HEAD_PAD:nq + (h + 1) * HEAD_PAD] * sin
        qh = qh * lax.rsqrt(jnp.sum(qh * qh, axis=-1, keepdims=True) * (1.0 / QK_DIM) + EPS)
        qh = qh * gqn_ref[:, sl] * qpost_ref[:, sl]
        if want_kv:
            put_split(q_ref, h, qh)
            kh = kvx[:, sl] + kpe
            kh = kh * lax.rsqrt(jnp.sum(kh * kh, axis=-1, keepdims=True) * (1.0 / QK_DIM) + EPS)
            put_split(k_ref, h, kh * gkn_ref[:, sl])
        else:
            q_ref[:, sl] = qh


def _mla_prep(zr, cos, sin, lw, qpost, *, want_kv, hp):
    t = zr.shape[0]
    tm = _tile(t, 512)
    tab_nblk = cos.shape[0] // tm
    nq = N_HEADS * HEAD_PAD
    in_specs = [pl.BlockSpec((tm, ZR_QA_BLK[0]), lambda i: (i, ZR_QA_BLK[1])),
                pl.BlockSpec((tm, ZR_KV_BLK[0]), lambda i: (i, ZR_KV_BLK[1])),
                pl.BlockSpec((tm, HEAD_PAD), lambda i: (i % tab_nblk, 0)),
                pl.BlockSpec((tm, HEAD_PAD), lambda i: (i % tab_nblk, 0)),
                _const_spec((1, Q_LORA)), _const_spec((1, KV_LORA)),
                _const_spec(lw["wq"][0].shape), _const_spec(lw["wq"][1].shape),
                _const_spec((1, nq)), _const_spec((1, nq)),
                _const_spec(lw["wkv"][0].shape), _const_spec(lw["wkv"][1].shape),
                _const_spec((1, nq))]
    args = [zr, zr, cos, sin, lw["g_qa"], lw["g_kva"], *lw["wq"], lw["g_qn"], qpost, *lw["wkv"], lw["g_kn"]]
    q_w, q_dt = (2 * nq, BF16) if want_kv else (nq, F32)
    out_specs = [pl.BlockSpec((tm, q_w), lambda i: (i, 0)),
                 pl.BlockSpec((tm, KV_LORA), lambda i: (i, 0)),
                 pl.BlockSpec((tm, HEAD_PAD), lambda i: (i, 0))]
    out_shape = [jax.ShapeDtypeStruct((t, q_w), q_dt),
                 jax.ShapeDtypeStruct((t, KV_LORA), F32),
                 jax.ShapeDtypeStruct((t, HEAD_PAD), F32)]
    if want_kv:
        out_specs += [pl.BlockSpec((tm, 2 * nq), lambda i: (i, 0)),
                      pl.BlockSpec((tm, nq), lambda i: (i, 0))]
        out_shape += [jax.ShapeDtypeStruct((t, 2 * nq), BF16),
                      jax.ShapeDtypeStruct((t, nq), BF16)]
    return pl.pallas_call(
        functools.partial(_mla_prep_kernel, want_kv=want_kv, hp=hp),
        grid=(t // tm,),
        in_specs=in_specs, out_specs=out_specs, out_shape=out_shape,
        compiler_params=_cparams("parallel"),
        name="mla_prep",
    )(*args)


def _mla_prompt_kernel(qi_ref, ki_ref, q_ref, k_ref, v_ref, o_ref, m_s, l_s, acc_s, *, tq, sub):
    t = pl.program_id(2)
    qi = qi_ref[t]
    ki = ki_ref[t]

    @pl.when(ki == 0)
    def _():
        m_s[...] = jnp.full(m_s.shape, -jnp.inf, F32)
        l_s[...] = jnp.zeros(l_s.shape, F32)
        acc_s[...] = jnp.zeros(acc_s.shape, F32)

    def step(diagonal):
        for j in range(2):
            base = 2 * j * HEAD_PAD
            for r in range(tq // sub):
                rows = slice(r * sub, (r + 1) * sub)
                n_k = (r + 1) * sub if diagonal else tq
                k_hi = k_ref[0:n_k, base:base + HEAD_PAD]
                k_lo = k_ref[0:n_k, base + HEAD_PAD:base + 2 * HEAD_PAD]
                vv = v_ref[0:n_k, :]
                q2 = q_ref[rows, base:base + 2 * HEAD_PAD]
                s = (lax.dot_general(q2, jnp.concatenate([k_hi, k_hi], axis=1), _NT,
                                     preferred_element_type=F32)
                     + lax.dot_general(q2[:, :HEAD_PAD], k_lo, _NT, preferred_element_type=F32))
                if diagonal:
                    row = r * sub + lax.broadcasted_iota(jnp.int32, (sub, n_k), 0)
                    col = lax.broadcasted_iota(jnp.int32, (sub, n_k), 1)
                    s = jnp.where(col <= row, s, -jnp.inf)
                m_prev = m_s[j, rows]
                m_new = jnp.maximum(m_prev, jnp.max(s, axis=-1, keepdims=True))
                alpha = jnp.exp(m_prev - m_new)
                p = jnp.exp(s - m_new)
                l_s[j, rows] = alpha * l_s[j, rows] + jnp.sum(p, axis=-1, keepdims=True)
                p_hi, p_lo = _split(p)
                pv = jnp.dot(p_hi, vv, preferred_element_type=F32)
                pv = pv[:, :HEAD_PAD] + (pv[:, HEAD_PAD:]
                                         + jnp.dot(p_lo, vv[:, :HEAD_PAD], preferred_element_type=F32))
                acc_s[j, rows] = alpha * acc_s[j, rows] + pv
                m_s[j, rows] = m_new

    @pl.when(ki < qi)
    def _():
        step(False)

    @pl.when(ki == qi)
    def _():
        step(True)
        lane = lax.broadcasted_iota(jnp.int32, (tq, 2 * V_HEAD), 1)
        o_ref[...] = jnp.where(lane < V_HEAD, acc_s[0] / l_s[0], acc_s[1] / l_s[1])


def _mla_prompt(q2, k2, v2, *, nseq, seqlen):
    tq = _tile(seqlen, 1024)
    sub = _tile(tq, 256)
    nq = seqlen // tq
    qi_np = np.array([i for i in range(nq) for _ in range(i + 1)], np.int32)
    ki_np = np.array([j for i in range(nq) for j in range(i + 1)], np.int32)
    n_tri = int(qi_np.shape[0])
    pair_w = 4 * HEAD_PAD
    grid_spec = pltpu.PrefetchScalarGridSpec(
        num_scalar_prefetch=2,
        grid=(nseq, N_HEADS // 2, n_tri),
        in_specs=[pl.BlockSpec((tq, pair_w), lambda b, hp_, t, qi, ki: (b * nq + qi[t], hp_)),
                  pl.BlockSpec((tq, pair_w), lambda b, hp_, t, qi, ki: (b * nq + ki[t], hp_)),
                  pl.BlockSpec((tq, 2 * HEAD_PAD), lambda b, hp_, t, qi, ki: (b * nq + ki[t], hp_))],
        out_specs=pl.BlockSpec((tq, 2 * V_HEAD), lambda b, hp_, t, qi, ki: (b * nq + qi[t], hp_)),
        scratch_shapes=[pltpu.VMEM((2, tq, 1), F32), pltpu.VMEM((2, tq, 1), F32),
                        pltpu.VMEM((2, tq, 2 * V_HEAD), F32)])
    return pl.pallas_call(
        functools.partial(_mla_prompt_kernel, tq=tq, sub=sub),
        grid_spec=grid_spec,
        out_shape=jax.ShapeDtypeStruct((nseq * seqlen, N_HEADS * V_HEAD), F32),
        compiler_params=_cparams("parallel", "parallel", "arbitrary"),
        name="mla_prompt",
    )(jnp.asarray(qi_np), jnp.asarray(ki_np), q2, k2, v2)


def _mla_sample_kernel(pt_ref, q_ref, ckv_hbm, kpe_hbm, cn_ref, kn_ref, wkt_ref, wktp_ref, fold_ref, wv_ref,
                       o_ref, lhs_s, qr_s, m_s, l_s, ctx_s, cbuf, kbuf, sem, *, layer, pps, cpp, n_new):
    b = pl.program_id(0)
    s_idx = pl.program_id(1)
    n_steps = pl.num_programs(1)
    n_rows = N_HEADS * n_new
    g = b * n_steps + s_idx
    slot = g % 2

    def page_copies(bb, ss, sl):
        cps = []
        for i in range(pps):
            page = pt_ref[bb, ss * pps + i]
            cps.append(pltpu.make_async_copy(ckv_hbm.at[layer, page],
                                             cbuf.at[sl, pl.ds(i * PAGE, PAGE), :], sem.at[0, sl]))
            cps.append(pltpu.make_async_copy(kpe_hbm.at[layer, page],
                                             kbuf.at[sl, :, pl.ds(i * PAGE, PAGE)], sem.at[1, sl]))
        return cps

    @pl.when(g == 0)
    def _():
        for cp in page_copies(b, s_idx, slot):
            cp.start()

    @pl.when(g + 1 < pl.num_programs(0) * n_steps)
    def _():
        for cp in page_copies((g + 1) // n_steps, (g + 1) % n_steps, 1 - slot):
            cp.start()

    for cp in page_copies(b, s_idx, slot):
        cp.wait()

    @pl.when(s_idx == 0)
    def _():
        q = q_ref[...]
        lane_head = lax.broadcasted_iota(jnp.int32, q.shape, 1) // HEAD_PAD
        qblk = jnp.concatenate([jnp.where(lane_head == h, q, 0.0) for h in range(N_HEADS)], axis=0)
        qblk = qblk.astype(BF16)
        lhs_s[0:N_HEADS * QK_NOPE, :] = wkt_ref[...]
        lhs_s[N_HEADS * QK_NOPE:, :] = jnp.dot(qblk, wktp_ref[...], preferred_element_type=F32).astype(BF16)
        qr_s[...] = jnp.dot(qblk, fold_ref[...], preferred_element_type=F32).astype(BF16)
        m_s[...] = jnp.full(m_s.shape, -jnp.inf, F32)
        l_s[...] = jnp.zeros(l_s.shape, F32)
        ctx_s[...] = jnp.zeros(ctx_s.shape, F32)

    def scores(c, sr, ssr):
        cb = c.astype(BF16)
        big = lax.dot_general(lhs_s[...], cb, _NT, preferred_element_type=F32)
        pieces = []
        for h in range(N_HEADS):
            kn = big[h * QK_NOPE:(h + 1) * QK_NOPE]
            ss = jnp.sum(kn * kn, axis=0, keepdims=True) + ssr
            rk = lax.rsqrt(ss * (1.0 / QK_DIM) + EPS)
            rows = slice(N_HEADS * QK_NOPE + h * n_new, N_HEADS * QK_NOPE + (h + 1) * n_new)
            pieces.append((big[rows] + sr[h * n_new:(h + 1) * n_new]) * rk)
        return cb, jnp.concatenate(pieces, axis=0)

    def update(chunks):
        m_new = m_s[...]
        for _, sc in chunks:
            m_new = jnp.maximum(m_new, jnp.max(sc, axis=-1, keepdims=True))
        alpha = jnp.exp(m_s[...] - m_new)
        l_new = alpha * l_s[...]
        ctx = alpha * ctx_s[...]
        for cb, sc in chunks:
            p = jnp.exp(sc - m_new)
            l_new = l_new + jnp.sum(p, axis=-1, keepdims=True)
            ctx = ctx + jnp.dot(p.astype(BF16), cb, preferred_element_type=F32)
        l_s[...] = l_new
        ctx_s[...] = ctx
        m_s[...] = m_new

    chunks = []
    for j in range(0, pps, cpp):
        keys = pl.ds(j * PAGE, cpp * PAGE)
        kpe_t = kbuf[slot, :, keys]
        chunks.append(scores(cbuf[slot, keys, :],
                             jnp.dot(qr_s[...], kpe_t.astype(BF16), preferred_element_type=F32),
                             jnp.sum(kpe_t * kpe_t, axis=0, keepdims=True)))
    update(chunks)

    @pl.when(s_idx == pl.num_programs(1) - 1)
    def _():
        cpad = jnp.concatenate([cn_ref[...], jnp.zeros((PAGE - n_new, KV_LORA), F32)], axis=0)
        kpad = jnp.concatenate([kn_ref[...], jnp.zeros((PAGE - n_new, QK_ROPE), F32)], axis=0)
        key = lax.broadcasted_iota(jnp.int32, (n_rows, PAGE), 1)
        qry = lax.broadcasted_iota(jnp.int32, (n_rows, PAGE), 0) % n_new
        sr = lax.dot_general(qr_s[...], kpad.astype(BF16), _NT, preferred_element_type=F32)
        ones = jnp.ones((8, QK_ROPE), BF16)
        ssr = lax.dot_general(ones, (kpad * kpad).astype(BF16), _NT, preferred_element_type=F32)[0:1]
        cb, sc = scores(cpad, sr, ssr)
        update([(cb, jnp.where(key <= qry, sc, -jnp.inf))])
        ctx = (ctx_s[...] / l_s[...]).astype(BF16)
        res = jnp.dot(ctx, wv_ref[...], preferred_element_type=F32)
        lane_head = lax.broadcasted_iota(jnp.int32, (n_new, N_HEADS * V_HEAD), 1) // V_HEAD
        out = jnp.zeros((n_new, N_HEADS * V_HEAD), F32)
        for h in range(N_HEADS):
            out = out + jnp.where(lane_head == h, res[h * n_new:(h + 1) * n_new], 0.0)
        o_ref[...] = out


def _mla_sample(q, c_new, kpe_new, cache_ckv, cache_kpe_t, page_table, layer, lw):
    nseq, n_new, _ = q.shape
    n_pages = page_table.shape[1]
    pps = _tile(n_pages, 64)
    n_steps = n_pages // pps
    n_rows = N_HEADS * n_new

    in_specs = [pl.BlockSpec((None, n_new, N_HEADS * HEAD_PAD), lambda b, s, pt: (b, 0, 0)),
                pl.BlockSpec(memory_space=pl.ANY), pl.BlockSpec(memory_space=pl.ANY),
                pl.BlockSpec((None, n_new, KV_LORA), lambda b, s, pt: (b, 0, 0)),
                pl.BlockSpec((None, n_new, QK_ROPE), lambda b, s, pt: (b, 0, 0)),
                _const_spec(lw["wkt"].shape), _const_spec(lw["wktp"].shape),
                _const_spec(lw["fold"].shape), _const_spec(lw["wv"].shape)]
    grid_spec = pltpu.PrefetchScalarGridSpec(
        num_scalar_prefetch=1,
        grid=(nseq, n_steps),
        in_specs=in_specs,
        out_specs=pl.BlockSpec((None, n_new, N_HEADS * V_HEAD), lambda b, s, pt: (b, 0, 0)),
        scratch_shapes=[pltpu.VMEM((N_HEADS * QK_NOPE + n_rows, KV_LORA), BF16),
                        pltpu.VMEM((n_rows, QK_ROPE), BF16),
                        pltpu.VMEM((n_rows, 1), F32), pltpu.VMEM((n_rows, 1), F32),
                        pltpu.VMEM((n_rows, KV_LORA), F32),
                        pltpu.VMEM((2, pps * PAGE, KV_LORA), F32),
                        pltpu.VMEM((2, QK_ROPE, pps * PAGE), F32),
                        pltpu.SemaphoreType.DMA((2, 2))])
    return pl.pallas_call(
        functools.partial(_mla_sample_kernel, layer=layer, pps=pps, cpp=_tile(pps, 4), n_new=n_new),
        grid_spec=grid_spec,
        out_shape=jax.ShapeDtypeStruct((nseq, n_new, N_HEADS * V_HEAD), F32),
        compiler_params=_cparams("arbitrary", "arbitrary"),
        name="mla_sample",
    )(page_table, q, cache_ckv, cache_kpe_t, c_new, kpe_new,
      lw["wkt"], lw["wktp"], lw["fold"], lw["wv"])


def _memkv_kernel(x_ref, g_ref, wh_ref, wl_ref, gk_ref, k_ref, v_ref, *, hp):
    kv = _mm(_rms(x_ref[...], g_ref[...]), wh_ref[...], wl_ref[...], hp)
    for h in range(MEM_HEADS):
        sl = slice(h * MEM_HD, (h + 1) * MEM_HD)
        k_ref[:, sl] = _rms(kv[:, sl], gk_ref[...])
    v_ref[...] = kv[:, MEM_W:]


def _memkv(mem, g_mem, wmkv_hl, g_mkn, *, hp):
    t = mem.shape[0]
    tm = _tile(t, 256)
    return pl.pallas_call(
        functools.partial(_memkv_kernel, hp=hp),
        grid=(t // tm,),
        in_specs=[pl.BlockSpec((tm, D_MODEL), lambda i: (i, 0)),
                  _const_spec((1, D_MODEL)), _const_spec(wmkv_hl[0].shape), _const_spec(wmkv_hl[1].shape),
                  _const_spec((1, MEM_HD))],
        out_specs=[pl.BlockSpec((tm, MEM_W), lambda i: (i, 0))] * 2,
        out_shape=[jax.ShapeDtypeStruct((t, MEM_W), F32)] * 2,
        compiler_params=_cparams("parallel"),
        name="memkv",
    )(mem, g_mem, *wmkv_hl, g_mkn)


def _memattn_kernel(q_ref, k_ref, v_ref, g_ref, o_ref, *, hp, rows_by_head):
    scale = 1.0 / math.sqrt(MEM_HD)
    for h in range(MEM_HEADS):
        sl = slice(h * MEM_HD, (h + 1) * MEM_HD)
        if rows_by_head:
            k = k_ref[pl.ds(h, N_MEM, stride=MEM_HEADS), :]
            v = v_ref[pl.ds(h, N_MEM, stride=MEM_HEADS), :]
        else:
            k = k_ref[:, sl]
            v = v_ref[:, sl]
        q = _rms(q_ref[:, sl], g_ref[...]) * scale
        s = _mm2(q, k, hp, _NT)
        p = jnp.exp(s - jnp.max(s, axis=-1, keepdims=True))
        l = jnp.sum(p, axis=-1, keepdims=True)
        o_ref[:, sl] = _mm2(p, v, hp) / l


def _memattn(zr, mk, mv, g_mqn, *, nseq, seqlen, kv_index, hp):
    tq = _tile(seqlen, 512)
    nq = seqlen // tq
    lead = (None,) * (mk.ndim - 2)
    kv_blk = (N_MEM, MEM_W) if mk.shape[-1] == MEM_W else (N_MEM * MEM_HEADS, MEM_HD)
    return pl.pallas_call(
        functools.partial(_memattn_kernel, hp=hp, rows_by_head=kv_blk[1] == MEM_HD),
        grid=(nseq, nq),
        in_specs=[pl.BlockSpec((tq, ZR_MQ_BLK[0]), lambda b, i: (b * nq + i, ZR_MQ_BLK[1])),
                  pl.BlockSpec(lead + kv_blk, lambda b, i: kv_index(b)),
                  pl.BlockSpec(lead + kv_blk, lambda b, i: kv_index(b)),
                  _const_spec((1, MEM_HD))],
        out_specs=pl.BlockSpec((tq, MEM_W), lambda b, i: (b * nq + i, 0)),
        out_shape=jax.ShapeDtypeStruct((nseq * seqlen, MEM_W), F32),
        compiler_params=_cparams("parallel", "parallel"),
        name="memattn",
    )(zr, mk, mv, g_mqn)


def _merge_kernel(x_ref, ya_ref, yb_ref, yc_ref, ym_ref, zg_ref, wbrh_ref, wbrl_ref, woh_ref, wol_ref,
                  gf_ref, wrh_ref, wrl_ref, br_ref, x1_ref, h2_ref, cw_ref, *, hp):
    acc = None
    for b, y_ref in enumerate((ya_ref, yb_ref, yc_ref, ym_ref)):
        term = zg_ref[:, b * D_MODEL:(b + 1) * D_MODEL] * _mm(y_ref[...], wbrh_ref[b], wbrl_ref[b], hp)
        acc = term if acc is None else acc + term
    x1 = x_ref[...] + _mm(acc, woh_ref[...], wol_ref[...], hp)
    x1_ref[...] = x1
    h2 = _rms(x1, gf_ref[...])
    h2_ref[...] = h2
    logits = _mm(h2, wrh_ref[...], wrl_ref[...], True) + br_ref[...]
    lane = lax.broadcasted_iota(jnp.int32, logits.shape, 1).astype(F32)
    big = 1e9
    is_grp = (lane >= N_EXPERTS) & (lane < N_EXPERTS + N_GROUPS)
    gl = jnp.where(is_grp, logits, -jnp.inf)
    ge = jnp.exp(gl - jnp.max(gl, axis=-1, keepdims=True))
    gprob = ge / jnp.sum(ge, axis=-1, keepdims=True)
    gp = jnp.max(gprob, axis=-1, keepdims=True)
    gi = jnp.min(jnp.where(is_grp & (gprob == gp), lane, big), axis=-1, keepdims=True) - N_EXPERTS
    sel = (lane >= gi * EXP_PER_GROUP) & (lane < (gi + 1) * EXP_PER_GROUP)
    el = jnp.where(sel, logits, -jnp.inf)
    ee = jnp.exp(el - jnp.max(el, axis=-1, keepdims=True))
    eprob = ee / jnp.sum(ee, axis=-1, keepdims=True)
    v1 = jnp.max(eprob, axis=-1, keepdims=True)
    i1 = jnp.min(jnp.where(sel & (eprob == v1), lane, big), axis=-1, keepdims=True)
    rest = jnp.where(sel & (lane != i1), eprob, -1.0)
    v2 = jnp.max(rest, axis=-1, keepdims=True)
    i2 = jnp.min(jnp.where(rest == v2, lane, big), axis=-1, keepdims=True)
    den = v1 + v2
    cw = jnp.where(lane == i1, gp * v1 / den, jnp.where(lane == i2, gp * v2 / den, 0.0))
    cw_ref[...] = jnp.where(lane == GROUP_LANE, gi, cw)


def _merge(x, ya, yb, yc, ym, zg, lw, *, hp):
    t = x.shape[0]
    tm = _tile(t, 256)
    row = lambda w: pl.BlockSpec((tm, w), lambda i: (i, 0))
    consts = [*lw["wbr"], *lw["wo"], lw["g_ffn"], *lw["wr"], lw["b_r"]]
    return pl.pallas_call(
        functools.partial(_merge_kernel, hp=hp),
        grid=(t // tm,),
        in_specs=[row(D_MODEL), row(POOL_W), row(512), row(SG_W), row(MEM_W), row(N_BRANCH * D_MODEL)]
                 + [_const_spec(c.shape) for c in consts],
        out_specs=[row(D_MODEL), row(D_MODEL), row(128)],
        out_shape=[jax.ShapeDtypeStruct((t, D_MODEL), F32),
                   jax.ShapeDtypeStruct((t, D_MODEL), F32),
                   jax.ShapeDtypeStruct((t, 128), F32)],
        compiler_params=_cparams("parallel"),
        name="merge",
    )(x, ya, yb, yc, ym, zg, *consts)


def _moe_kernel(h_ref, cw_ref, x_ref, wguh_ref, wgul_ref, wdh_ref, wdl_ref, o_ref, acc_s, hh_s, hl_s, *, hp):
    e = pl.program_id(1)

    @pl.when(e == 0)
    def _():
        acc_s[...] = jnp.zeros(acc_s.shape, F32)
        hh_s[...], hl_s[...] = _split(h_ref[...])

    dot = lambda a, b: jnp.dot(a, b, preferred_element_type=F32)
    ab = dot(hh_s[...], wguh_ref[...])
    if hp:
        ab = ab + (dot(hl_s[...], wguh_ref[...]) + dot(hh_s[...], wgul_ref[...]))
    cw = cw_ref[...]
    lane = lax.broadcasted_iota(jnp.int32, cw.shape, 1)
    cwe = jnp.sum(jnp.where(lane == e, cw, 0.0), axis=-1, keepdims=True)
    t = jax.nn.silu(ab[:, :D_EXPERT]) * ab[:, D_EXPERT:] * cwe
    acc_s[...] += _mm(t, wdh_ref[...], wdl_ref[...], hp)

    @pl.when(e == N_EXPERTS - 1)
    def _():
        o_ref[...] = x_ref[...] + acc_s[...]


def _moe(h2, cw, x1, wgu_hl, wd_hl, *, hp):
    t = h2.shape[0]
    tm = _tile(t, 512)
    wgu_spec = pl.BlockSpec((None, D_MODEL, 2 * D_EXPERT), lambda i, e: (e, 0, 0))
    wd_spec = pl.BlockSpec((None, D_EXPERT, D_MODEL), lambda i, e: (e, 0, 0))
    return pl.pallas_call(
        functools.partial(_moe_kernel, hp=hp),
        grid=(t // tm, N_EXPERTS),
        in_specs=[pl.BlockSpec((tm, D_MODEL), lambda i, e: (i, 0)),
                  pl.BlockSpec((tm, 128), lambda i, e: (i, 0)),
                  pl.BlockSpec((tm, D_MODEL), lambda i, e: (i, 0)),
                  wgu_spec, wgu_spec, wd_spec, wd_spec],
        out_specs=pl.BlockSpec((tm, D_MODEL), lambda i, e: (i, 0)),
        out_shape=jax.ShapeDtypeStruct((t, D_MODEL), F32),
        scratch_shapes=[pltpu.VMEM((tm, D_MODEL), F32), pltpu.VMEM((tm, D_MODEL), BF16),
                        pltpu.VMEM((tm, D_MODEL), BF16)],
        compiler_params=_cparams("parallel", "arbitrary"),
        name="moe",
    )(h2, cw, x1, *wgu_hl, *wd_hl)


def _split3(a):
    mask = lambda v: lax.bitcast_convert_type(lax.bitcast_convert_type(v, jnp.int32) & jnp.int32(-65536), F32)
    p1 = mask(a)
    r1 = a - p1
    p2 = mask(r1)
    return p1.astype(BF16), p2.astype(BF16), (r1 - p2).astype(BF16)


def _moe_sorted_kernel(h_ref, cw_ref, x_ref, wguh_ref, wgul_ref, wdh_ref, wdl_ref, o_ref,
                       xh_s, xl_s, cws_s, ys_s, pt_s, seg_s, *, tb, ch):
    e = pl.program_id(1)
    dot = lambda a, b: jnp.dot(a, b, preferred_element_type=F32)

    @pl.when(e == 0)
    def _():
        cw = cw_ref[...]
        lane = lax.broadcasted_iota(jnp.int32, (tb, 128), 1)
        gi = jnp.sum(jnp.where(lane == GROUP_LANE, cw, 0.0), axis=-1, keepdims=True)
        oh = jnp.where(lane.astype(F32) == gi, 1.0, 0.0)
        r = lax.broadcasted_iota(jnp.int32, (tb, tb), 0)
        c = lax.broadcasted_iota(jnp.int32, (tb, tb), 1)
        lower = jnp.where(c < r, 1.0, 0.0).astype(BF16)
        rank = dot(lower, oh.astype(BF16))
        cnt = jnp.sum(oh, axis=0, keepdims=True)
        lane_row = lax.broadcasted_iota(jnp.int32, (1, 128), 1)
        off = jnp.zeros((1, 128), F32)
        for g in range(N_GROUPS - 1):
            off = off + jnp.where(lane_row > g, cnt[:, g:g + 1], 0.0)
        slot = jnp.sum(oh * (off + rank), axis=-1, keepdims=True)
        pt_s[...] = jnp.where(slot == c.astype(F32), 1.0, 0.0).astype(BF16)
        d_hi = jnp.floor(slot * (1.0 / 32.0))
        digits = jnp.where(lane == 0, d_hi, jnp.where(lane == 1, slot - 32.0 * d_hi, 0.0)).astype(BF16)
        er = lax.broadcasted_iota(jnp.int32, (128, 128), 0)
        ec = lax.broadcasted_iota(jnp.int32, (128, 128), 1)
        eye = jnp.where(er == ec, 1.0, 0.0).astype(BF16)
        dt = lax.dot_general(eye, digits, _NT, preferred_element_type=F32)
        slot_row = 32.0 * dt[0:1] + dt[1:2]
        p = jnp.where(slot_row == r.astype(F32), 1.0, 0.0).astype(BF16)
        h_hi, h_lo = _split(h_ref[...])
        xh_s[...] = dot(p, h_hi).astype(BF16)
        xl_s[...] = dot(p, h_lo).astype(BF16)
        c1, c2, c3 = _split3(cw)
        cws_s[...] = dot(p, c1) + (dot(p, c2) + dot(p, c3))
        ys_s[...] = jnp.zeros(ys_s.shape, F32)
        for g in range(N_GROUPS):
            seg_s[g] = off[0, g].astype(jnp.int32)
            seg_s[N_GROUPS + g] = (off[0, g] + cnt[0, g]).astype(jnp.int32)

    grp = e // EXP_PER_GROUP
    start = seg_s[grp]
    end = seg_s[N_GROUPS + grp]
    for c in range(tb // ch):
        @pl.when((start < (c + 1) * ch) & (end > c * ch))
        def _(c=c):
            rows = slice(c * ch, (c + 1) * ch)
            xh = xh_s[rows]
            ab = dot(xh, wguh_ref[...]) + (dot(xl_s[rows], wguh_ref[...]) + dot(xh, wgul_ref[...]))
            cws = cws_s[rows]
            lane = lax.broadcasted_iota(jnp.int32, cws.shape, 1)
            cwe = jnp.sum(jnp.where(lane == e, cws, 0.0), axis=-1, keepdims=True)
            t = jax.nn.silu(ab[:, :D_EXPERT]) * ab[:, D_EXPERT:] * cwe
            ys_s[rows] += _mm(t, wdh_ref[...], wdl_ref[...], True)

    @pl.when(e == N_EXPERTS - 1)
    def _():
        y1, y2, y3 = _split3(ys_s[...])
        pt = pt_s[...]
        o_ref[...] = x_ref[...] + (dot(pt, y1) + (dot(pt, y2) + dot(pt, y3)))


def _moe_sorted(h2, cw, x1, wgu_hl, wd_hl):
    t = h2.shape[0]
    tb = 1024
    once = lambda w: pl.BlockSpec((tb, w), lambda i, e: (i, 0), pipeline_mode=pl.Buffered(1))
    wgu_spec = pl.BlockSpec((None, D_MODEL, 2 * D_EXPERT), lambda i, e: (e, 0, 0))
    wd_spec = pl.BlockSpec((None, D_EXPERT, D_MODEL), lambda i, e: (e, 0, 0))
    return pl.pallas_call(
        functools.partial(_moe_sorted_kernel, tb=tb, ch=256),
        grid=(t // tb, N_EXPERTS),
        in_specs=[once(D_MODEL), once(128), once(D_MODEL), wgu_spec, wgu_spec, wd_spec, wd_spec],
        out_specs=pl.BlockSpec((tb, D_MODEL), lambda i, e: (i, 0)),
        out_shape=jax.ShapeDtypeStruct((t, D_MODEL), F32),
        scratch_shapes=[pltpu.VMEM((tb, D_MODEL), BF16), pltpu.VMEM((tb, D_MODEL), BF16),
                        pltpu.VMEM((tb, 128), F32), pltpu.VMEM((tb, D_MODEL), F32),
                        pltpu.VMEM((tb, tb), BF16), pltpu.SMEM((2 * N_GROUPS,), jnp.int32)],
        compiler_params=_cparams("parallel", "arbitrary"),
        name="moe_sorted",
    )(h2, cw, x1, *wgu_hl, *wd_hl)


def _hl(w):
    b = lax.bitcast_convert_type(w, jnp.int32)
    b = (b + jnp.int32(0x7FFF) + ((b >> 16) & 1)) & jnp.int32(-65536)
    hi = lax.bitcast_convert_type(b, F32)
    return hi.astype(BF16), (w - hi).astype(BF16)


def _rope_tables(pos):
    half = QK_ROPE // 2
    inv = ROPE_THETA ** (-jnp.arange(half, dtype=F32) / half)
    ang = pos.astype(F32)[:, None] * inv
    cos, sin = jnp.cos(ang), jnp.sin(ang)
    n = pos.shape[0]
    cos128 = jnp.concatenate([jnp.ones((n, QK_NOPE), F32), cos, cos, jnp.zeros((n, 32), F32)], 1)
    sin128 = jnp.concatenate([jnp.zeros((n, QK_NOPE), F32), sin, sin, jnp.zeros((n, 32), F32)], 1)
    return cos128, sin128


def _fold_matrix():
    f = np.zeros((N_HEADS * HEAD_PAD, QK_ROPE), np.float32)
    for h in range(N_HEADS):
        for j in range(QK_ROPE):
            f[h * HEAD_PAD + QK_NOPE + j, j] = 1.0
    return jnp.asarray(f, BF16)


def _head_pad_gain(g):
    return jnp.tile(jnp.concatenate([g, jnp.zeros((HEAD_PAD - QK_DIM,), F32)]), N_HEADS)[None]


def _layer_weights(l, w):
    half = QK_ROPE // 2
    wi = w["w_in"][l]
    w_a, w_qa = wi[:, 0:512], wi[:, 512:768]
    w_lat, w_rope = wi[:, 768:896], wi[:, 896:928]
    w_c, w_mq, w_g = wi[:, 928:1952], wi[:, 1952:2464], wi[:, 2464:]
    z64 = jnp.zeros((D_MODEL, QK_NOPE), F32)
    z32 = jnp.zeros((D_MODEL, 32), F32)
    rope128 = jnp.concatenate([z64, w_rope, z32], 1)
    ropesw128 = jnp.concatenate([z64, -w_rope[:, half:], w_rope[:, :half], z32], 1)
    lw = {}
    lw["wr_in"] = _hl(jnp.concatenate([w_c, w_a, w_mq, w_qa, w_lat, rope128, ropesw128], 1))
    lw["wg_in"] = _hl(w_g)
    lw["g_mix"] = w["g_mix"][l][None]

    wq = w["w_qb"][l].reshape(Q_LORA, N_HEADS, QK_DIM)
    nope, x1, x2 = wq[..., :QK_NOPE], wq[..., QK_NOPE:QK_NOPE + half], wq[..., QK_NOPE + half:]
    zq = lambda n: jnp.zeros((Q_LORA, N_HEADS, n), F32)
    wq1 = jnp.concatenate([nope, x1, x2, zq(32)], -1).reshape(Q_LORA, N_HEADS * HEAD_PAD)
    wq2 = jnp.concatenate([zq(QK_NOPE), -x2, x1, zq(32)], -1).reshape(Q_LORA, N_HEADS * HEAD_PAD)
    lw["wq"] = _hl(jnp.concatenate([wq1, wq2], 1))
    lw["g_qa"] = w["g_qa"][l][None]
    lw["g_kva"] = w["g_kva"][l][None]
    lw["g_qn"] = _head_pad_gain(w["g_qn"][l])
    lw["g_kn"] = _head_pad_gain(w["g_kn"][l])

    wkv = w["w_kvb"][l].reshape(KV_LORA, N_HEADS, QK_NOPE + V_HEAD)
    kn, vv = wkv[..., :QK_NOPE], wkv[..., QK_NOPE:]
    wk_pad = jnp.concatenate([kn, jnp.zeros((KV_LORA, N_HEADS, HEAD_PAD - QK_NOPE), F32)], -1)
    wk_pad = wk_pad.reshape(KV_LORA, N_HEADS * HEAD_PAD)
    wv = vv.reshape(KV_LORA, N_HEADS * V_HEAD)
    lw["wkv"] = _hl(jnp.concatenate([wk_pad, wv], 1))
    lw["wkt"] = kn.reshape(KV_LORA, N_HEADS * QK_NOPE).T.astype(BF16)
    lw["wktp"] = wk_pad.T.astype(BF16)
    lw["wv"] = wv.astype(BF16)
    lw["fold"] = _fold_matrix()

    lw["w_pool"] = _hl(w["w_pool"][l])
    lw["pool_scale"] = w["pool_scale"][l][None]
    lw["g_sgv"] = w["g_sgv"][l]
    lw["w_sp"] = w["w_sp"][l]
    lw["b_sp"] = w["b_sp"][l]
    lw["g_mem"] = w["g_mem"][l][None]
    lw["wmkv"] = _hl(jnp.concatenate([w["w_mk"][l], w["w_mv"][l]], 1))
    lw["g_mqn"] = w["g_mqn"][l][None]
    lw["g_mkn"] = w["g_mkn"][l][None]
    lw["wbr"] = _hl(w["w_br"][l])
    lw["wo"] = _hl(w["w_o"][l])
    lw["g_ffn"] = w["g_ffn"][l][None]
    lw["wr"] = _hl(jnp.concatenate([w["w_re"][l], w["w_rg"][l],
                                    jnp.zeros((D_MODEL, 128 - N_EXPERTS - N_GROUPS), F32)], 1))
    lw["b_r"] = jnp.concatenate([w["b_re"][l], w["b_rg"][l],
                                 jnp.zeros((128 - N_EXPERTS - N_GROUPS,), F32)])[None]
    lw["wgu"] = _hl(jnp.concatenate([w["w_eg"][l], w["w_eu"][l]], -1))
    lw["wd"] = _hl(w["w_ed"][l])
    return lw


def _sgu_operands(lw, seqlen):
    blk = min(seqlen, CHUNK)
    rep = CHUNK // blk
    ws = jnp.tile(lw["w_sp"][:, :blk, :blk], (1, rep, rep))
    b = jnp.tile(lw["b_sp"][:, :blk], (1, rep))
    bfull = jnp.repeat(b.T, CHUNK, axis=1)
    return ws, bfull, blk


def _group_layer(x, lw, *, nseq, seqlen, pos0, cos, sin, pool_buf, mem_k, mem_v, kv_index,
                 mla_sample_fn, hp):
    zr = _inproj(x, lw["g_mix"], lw["wr_in"], hp=hp, sigmoid=False)
    zg = _inproj(x, lw["g_mix"], lw["wg_in"], hp=hp, sigmoid=True)
    ya, pool_new = _pool(zr, pool_buf, lw["w_pool"], lw["pool_scale"],
                         nseq=nseq, seqlen=seqlen, pos0=pos0, hp=hp)
    ws, bfull, blk = _sgu_operands(lw, seqlen)
    yc, v_rows = _sgu(zr, lw["g_sgv"], ws, bfull, blk=blk, hp=hp)
    scale = 1.0 / math.sqrt(QK_DIM)
    if mla_sample_fn is not None:
        q, c, kpe128 = _mla_prep(zr, cos, sin, lw, lw["g_kn"] * scale, want_kv=False, hp=hp)
        kpe = kpe128[:, QK_NOPE:QK_DIM]
        yb = mla_sample_fn(q, c, kpe)
    else:
        qpost = jnp.full((1, N_HEADS * HEAD_PAD), scale, F32)
        q, c, kpe128, k, v = _mla_prep(zr, cos, sin, lw, qpost, want_kv=True, hp=hp)
        kpe = kpe128[:, QK_NOPE:QK_DIM]
        yb = _mla_prompt(q, k, v, nseq=nseq, seqlen=seqlen)
    ym = _memattn(zr, mem_k, mem_v, lw["g_mqn"], nseq=nseq, seqlen=seqlen, kv_index=kv_index, hp=hp)
    x1, h2, cw = _merge(x, ya, yb, yc, ym, zg, lw, hp=hp)
    if hp and x.shape[0] % 1024 == 0:
        x2 = _moe_sorted(h2, cw, x1, lw["wgu"], lw["wd"])
    else:
        x2 = _moe(h2, cw, x1, lw["wgu"], lw["wd"], hp=hp)
    return x2, c, kpe, pool_new, v_rows


def kernel(x_prompt, x_sample, mem_prompt, cache_ckv, cache_kpe, cache_memk, cache_memv, state_pool, page_table, g_mix, w_in, g_qa, w_qb, g_kva, w_kvb, g_qn, g_kn, w_pool, pool_scale, g_sgv, w_sp, b_sp, g_mem, w_mk, w_mv, g_mqn, g_mkn, w_br, w_o, g_ffn, w_rg, b_rg, w_re, b_re, w_eg, w_eu, w_ed):
    w = dict(g_mix=g_mix, w_in=w_in, g_qa=g_qa, w_qb=w_qb, g_kva=g_kva, w_kvb=w_kvb, g_qn=g_qn, g_kn=g_kn,
             w_pool=w_pool, pool_scale=pool_scale, g_sgv=g_sgv, w_sp=w_sp, b_sp=b_sp, g_mem=g_mem,
             w_mk=w_mk, w_mv=w_mv, g_mqn=g_mqn, g_mkn=g_mkn, w_br=w_br, w_o=w_o, g_ffn=g_ffn,
             w_rg=w_rg, b_rg=b_rg, w_re=w_re, b_re=b_re, w_eg=w_eg, w_eu=w_eu, w_ed=w_ed)
    nb, seq, _ = x_prompt.shape
    ndb, t_new, _ = x_sample.shape
    depth = w_in.shape[0]
    past = page_table.shape[1] * PAGE

    cos_p, sin_p = _rope_tables(jnp.arange(seq, dtype=jnp.int32))
    cos_s, sin_s = _rope_tables(past + jnp.arange(t_new, dtype=jnp.int32))
    tm_s = _tile(ndb * t_new, 512)
    cos_s = jnp.tile(cos_s, (tm_s // t_new, 1))
    sin_s = jnp.tile(sin_s, (tm_s // t_new, 1))

    xp = x_prompt.reshape(nb * seq, D_MODEL)
    xs = x_sample.reshape(ndb * t_new, D_MODEL)
    mem = mem_prompt.reshape(nb * N_MEM, D_MODEL)
    zero_buf = jnp.zeros((nb, POOL_PAD, POOL_W), F32)
    memk_s = cache_memk.reshape(depth, ndb, N_MEM * MEM_HEADS, MEM_HD)
    memv_s = cache_memv.reshape(depth, ndb, N_MEM * MEM_HEADS, MEM_HD)
    cache_kpe_t = jnp.swapaxes(cache_kpe, 2, 3)

    outs = [[] for _ in range(9)]
    for l in range(depth):
        lw = _layer_weights(l, w)
        mk, mv = _memkv(mem, lw["g_mem"], lw["wmkv"], lw["g_mkn"], hp=True)
        xp, c, kp, pb, _ = _group_layer(
            xp, lw, nseq=nb, seqlen=seq, pos0=0, cos=cos_p, sin=sin_p, pool_buf=zero_buf,
            mem_k=mk, mem_v=mv, kv_index=lambda b: (b, 0), mla_sample_fn=None, hp=True)
        outs[0].append(c.reshape(nb, seq, KV_LORA))
        outs[1].append(kp.reshape(nb, seq, QK_ROPE))
        outs[2].append(mk.reshape(nb, N_MEM, MEM_HEADS, MEM_HD))
        outs[3].append(mv.reshape(nb, N_MEM, MEM_HEADS, MEM_HD))
        outs[4].append(pb[:, 1:])

        def mla_sample_fn(q, c_new, kpe_new, l=l, lw=lw):
            o = _mla_sample(q.reshape(ndb, t_new, -1), c_new.reshape(ndb, t_new, KV_LORA),
                            kpe_new.reshape(ndb, t_new, QK_ROPE), cache_ckv, cache_kpe_t, page_table, l, lw)
            return o.reshape(ndb * t_new, N_HEADS * V_HEAD)

        buf_s = jnp.pad(state_pool[l], ((0, 0), (1, 0), (0, 0)))
        xs, c, kp, pb, vr = _group_layer(
            xs, lw, nseq=ndb, seqlen=t_new, pos0=past, cos=cos_s, sin=sin_s, pool_buf=buf_s,
            mem_k=memk_s, mem_v=memv_s, kv_index=lambda b, l=l: (l, b, 0, 0),
            mla_sample_fn=mla_sample_fn, hp=False)
        outs[5].append(c.reshape(ndb, t_new, KV_LORA))
        outs[6].append(kp.reshape(ndb, t_new, QK_ROPE))
        outs[7].append(pb[:, 1:])
        outs[8].append(vr.reshape(ndb, t_new, SG_W))

    stacked = [jnp.stack(o) for o in outs]
    return (xp.reshape(nb, seq, D_MODEL), xs.reshape(ndb, t_new, D_MODEL), *stacked)
```

```python
import functools
import math

import numpy as np
import jax
import jax.numpy as jnp
from jax import lax
from jax.experimental import pallas as pl
from jax.experimental.pallas import tpu as pltpu

F32 = jnp.float32
BF16 = jnp.bfloat16
EPS = 1e-6

D_MODEL = 1024
PAGE = 128
N_MEM = 256
POOL_WINDOWS = (2, 4, 8, 16)
POOL_GW = 128
POOL_W = 512
POOL_BUF = 15
POOL_PAD = 16
N_HEADS = 8
QK_NOPE = 64
QK_ROPE = 32
QK_DIM = QK_NOPE + QK_ROPE
V_HEAD = 64
HEAD_PAD = 128
Q_LORA = 256
KV_LORA = 128
ROPE_THETA = 10000.0
CHUNK = 128
SG_GROUPS = 4
SG_W = 512
MEM_HEADS = 4
MEM_HD = 128
MEM_W = 512
N_BRANCH = 4
N_GROUPS = 4
EXP_PER_GROUP = 4
N_EXPERTS = 16
D_EXPERT = 256
GROUP_LANE = N_EXPERTS

ZR_W = 2688
ZR_C_BLK = (1024, 0)
ZR_A_BLK = (512, 2)
ZR_MQ_BLK = (512, 3)
ZR_QA_BLK = (256, 8)
ZR_KV_BLK = (384, 6)

VMEM_LIMIT = 56 * 1024 * 1024

_NN = (((1,), (0,)), ((), ()))
_NT = (((1,), (1,)), ((), ()))


def _tile(n, pref):
    return pref if n % pref == 0 else n


def _cparams(*sem):
    return pltpu.CompilerParams(dimension_semantics=sem, vmem_limit_bytes=VMEM_LIMIT)


def _rms(x, g):
    return x * lax.rsqrt(jnp.mean(x * x, axis=-1, keepdims=True) + EPS) * g


def _const_spec(shape):
    nd = len(shape)
    return pl.BlockSpec(shape, lambda *_: (0,) * nd, pipeline_mode=pl.Buffered(1))


def _split(a):
    hi = lax.bitcast_convert_type(lax.bitcast_convert_type(a, jnp.int32) & jnp.int32(-65536), F32)
    return hi.astype(BF16), (a - hi).astype(BF16)


def _mm(a, w_hi, w_lo, hp, dims=_NN):
    dot = lambda x, y: lax.dot_general(x, y, dims, preferred_element_type=F32)
    if not hp:
        return dot(a.astype(BF16), w_hi)
    a_hi, a_lo = _split(a)
    return dot(a_hi, w_hi) + (dot(a_lo, w_hi) + dot(a_hi, w_lo))


def _mm2(a, b, hp, dims=_NN):
    if not hp:
        return lax.dot_general(a.astype(BF16), b.astype(BF16), dims, preferred_element_type=F32)
    b_hi, b_lo = _split(b)
    return _mm(a, b_hi, b_lo, True, dims)


def _inproj_kernel(x_ref, g_ref, wh_ref, wl_ref, z_ref, *, hp, sigmoid):
    h = _rms(x_ref[...], g_ref[...])
    z = _mm(h, wh_ref[...], wl_ref[...], hp)
    z_ref[...] = jax.nn.sigmoid(z) if sigmoid else z


def _inproj(x, g, w_hl, *, hp, sigmoid):
    t = x.shape[0]
    tm = _tile(t, 256)
    n = w_hl[0].shape[1]
    return pl.pallas_call(
        functools.partial(_inproj_kernel, hp=hp, sigmoid=sigmoid),
        grid=(t // tm,),
        in_specs=[pl.BlockSpec((tm, D_MODEL), lambda i: (i, 0)),
                  _const_spec((1, D_MODEL)),
                  _const_spec(w_hl[0].shape), _const_spec(w_hl[1].shape)],
        out_specs=pl.BlockSpec((tm, n), lambda i: (i, 0)),
        out_shape=jax.ShapeDtypeStruct((t, n), F32),
        compiler_params=_cparams("parallel"),
        name="inproj",
    )(x, g, *w_hl)


def _pool_kernel(a_ref, buf_ref, wph_ref, wpl_ref, ps_ref, y_ref, pn_ref, ext_ref, *, nb, ts, pos0, hp):
    s = pl.program_id(1)

    @pl.when(s == 0)
    def _():
        ext_ref[:, 0:POOL_PAD, :] = buf_ref[...]

    a = a_ref[...].reshape(nb, ts, POOL_W)
    ext_ref[:, POOL_PAD:POOL_PAD + ts, :] = a
    pos = pos0 + s * ts + lax.broadcasted_iota(jnp.int32, (1, ts, 1), 1)
    for g, w in enumerate(POOL_WINDOWS):
        sl = slice(g * POOL_GW, (g + 1) * POOL_GW)
        acc = a[:, :, sl]
        for k in range(1, w):
            acc = acc + ext_ref[:, POOL_PAD - k:POOL_PAD - k + ts, sl]
        cnt = jnp.minimum(pos + 1, w).astype(F32)
        m = (acc / cnt - a[:, :, sl]).reshape(nb * ts, POOL_GW)
        y_ref[:, sl] = _mm(m, wph_ref[g], wpl_ref[g], hp) * ps_ref[:, sl]
    tail = ext_ref[:, ts:ts + POOL_PAD, :]
    pn_ref[...] = tail
    ext_ref[:, 0:POOL_PAD, :] = tail


def _pool(zr, buf, wp_hl, ps, *, nseq, seqlen, pos0, hp):
    if seqlen >= 512:
        nb, ts = 1, 512
    else:
        nb, ts = _tile(nseq, 16), seqlen
    n_s = seqlen // ts
    kern = functools.partial(_pool_kernel, nb=nb, ts=ts, pos0=pos0, hp=hp)
    return pl.pallas_call(
        kern,
        grid=(nseq // nb, n_s),
        in_specs=[pl.BlockSpec((nb * ts, POOL_W), lambda b, s: (b * n_s + s, ZR_A_BLK[1])),
                  pl.BlockSpec((nb, POOL_PAD, POOL_W), lambda b, s: (b, 0, 0)),
                  _const_spec(wp_hl[0].shape), _const_spec(wp_hl[1].shape),
                  _const_spec((1, POOL_W))],
        out_specs=[pl.BlockSpec((nb * ts, POOL_W), lambda b, s: (b * n_s + s, 0)),
                   pl.BlockSpec((nb, POOL_PAD, POOL_W), lambda b, s: (b, 0, 0))],
        out_shape=[jax.ShapeDtypeStruct((nseq * seqlen, POOL_W), F32),
                   jax.ShapeDtypeStruct((nseq, POOL_PAD, POOL_W), F32)],
        scratch_shapes=[pltpu.VMEM((nb, POOL_PAD + ts, POOL_W), F32)],
        compiler_params=_cparams("parallel", "arbitrary"),
        name="pool",
    )(zr, buf, *wp_hl, ps)


def _sgu_kernel(zc_ref, g_ref, ws_ref, b_ref, y_ref, v_ref, *, n_chunk, blk, hp):
    z = jax.nn.gelu(zc_ref[...])
    r = lax.broadcasted_iota(jnp.int32, (CHUNK, CHUNK), 0)
    c = lax.broadcasted_iota(jnp.int32, (CHUNK, CHUNK), 1)
    mask = (c <= r) & ((r // blk) == (c // blk))
    for g in range(SG_GROUPS):
        sl = slice(g * CHUNK, (g + 1) * CHUNK)
        vg = _rms(z[:, SG_W + g * CHUNK:SG_W + (g + 1) * CHUNK], g_ref[g:g + 1, :])
        v_ref[:, sl] = vg
        ws = jnp.where(mask, ws_ref[g], 0.0)
        for ch in range(n_chunk):
            rows = slice(ch * CHUNK, (ch + 1) * CHUNK)
            sp = _mm2(ws, vg[rows], hp) + b_ref[:, sl]
            y_ref[rows, sl] = z[rows, sl] * sp


def _sgu(zr, g_sgv, ws, bfull, *, blk, hp):
    t = zr.shape[0]
    rows = _tile(t, 512)
    n_chunk = rows // CHUNK
    return pl.pallas_call(
        functools.partial(_sgu_kernel, n_chunk=n_chunk, blk=blk, hp=hp),
        grid=(t // rows,),
        in_specs=[pl.BlockSpec((rows, ZR_C_BLK[0]), lambda i: (i, ZR_C_BLK[1])),
                  _const_spec(g_sgv.shape),
                  _const_spec(ws.shape),
                  _const_spec(bfull.shape)],
        out_specs=[pl.BlockSpec((rows, SG_W), lambda i: (i, 0))] * 2,
        out_shape=[jax.ShapeDtypeStruct((t, SG_W), F32)] * 2,
        compiler_params=_cparams("parallel"),
        name="sgu",
    )(zr, g_sgv, ws, bfull)


def _mla_prep_kernel(qa_ref, kv_ref, cos_ref, sin_ref, gqa_ref, gkva_ref, wqh_ref, wql_ref, gqn_ref,
                     qpost_ref, wkvh_ref, wkvl_ref, gkn_ref, q_ref, c_ref, kpe_ref, *maybe_kv,
                     want_kv, hp):
    cos = cos_ref[...]
    sin = sin_ref[...]
    qn = _rms(qa_ref[...], gqa_ref[...])
    qq = _mm(qn, wqh_ref[...], wql_ref[...], hp)
    kv = kv_ref[...]
    c = _rms(kv[:, :KV_LORA], gkva_ref[...])
    c_ref[...] = c
    kpe = kv[:, KV_LORA:2 * KV_LORA] * cos + kv[:, 2 * KV_LORA:] * sin
    kpe_ref[...] = kpe
    nq = N_HEADS * HEAD_PAD

    def put_split(ref, idx, val):
        hi, lo = _split(val)
        ref[:, 2 * idx * HEAD_PAD:(2 * idx + 1) * HEAD_PAD] = hi
        ref[:, (2 * idx + 1) * HEAD_PAD:(2 * idx + 2) * HEAD_PAD] = lo

    if want_kv:
        k_ref, v_ref = maybe_kv
        kvx = _mm(c, wkvh_ref[...], wkvl_ref[...], hp)
        for pair in range(N_HEADS // 2):
            put_split(v_ref, pair, kvx[:, nq + pair * HEAD_PAD:nq + (pair + 1) * HEAD_PAD])
    for h in range(N_HEADS):
        sl = slice(h * HEAD_PAD, (h + 1) * HEAD_PAD)
        qh = qq[:, sl] * cos + qq[:, nq + h * HEAD_PAD:nq + (h + 1) * HEAD_PAD] * sin
        qh = qh * lax.rsqrt(jnp.sum(qh * qh, axis=-1, keepdims=True) * (1.0 / QK_DIM) + EPS)
        qh = qh * gqn_ref[:, sl] * qpost_ref[:, sl]
        if want_kv:
            put_split(q_ref, h, qh)
            kh = kvx[:, sl] + kpe
            kh = kh * lax.rsqrt(jnp.sum(kh * kh, axis=-1, keepdims=True) * (1.0 / QK_DIM) + EPS)
            put_split(k_ref, h, kh * gkn_ref[:, sl])
        else:
            q_ref[:, sl] = qh


def _mla_prep(zr, cos, sin, lw, qpost, *, want_kv, hp):
    t = zr.shape[0]
    tm = _tile(t, 512)
    tab_nblk = cos.shape[0] // tm
    nq = N_HEADS * HEAD_PAD
    in_specs = [pl.BlockSpec((tm, ZR_QA_BLK[0]), lambda i: (i, ZR_QA_BLK[1])),
                pl.BlockSpec((tm, ZR_KV_BLK[0]), lambda i: (i, ZR_KV_BLK[1])),
                pl.BlockSpec((tm, HEAD_PAD), lambda i: (i % tab_nblk, 0)),
                pl.BlockSpec((tm, HEAD_PAD), lambda i: (i % tab_nblk, 0)),
                _const_spec((1, Q_LORA)), _const_spec((1, KV_LORA)),
                _const_spec(lw["wq"][0].shape), _const_spec(lw["wq"][1].shape),
                _const_spec((1, nq)), _const_spec((1, nq)),
                _const_spec(lw["wkv"][0].shape), _const_spec(lw["wkv"][1].shape),
                _const_spec((1, nq))]
    args = [zr, zr, cos, sin, lw["g_qa"], lw["g_kva"], *lw["wq"], lw["g_qn"], qpost, *lw["wkv"], lw["g_kn"]]
    q_w, q_dt = (2 * nq, BF16) if want_kv else (nq, F32)
    out_specs = [pl.BlockSpec((tm, q_w), lambda i: (i, 0)),
                 pl.BlockSpec((tm, KV_LORA), lambda i: (i, 0)),
                 pl.BlockSpec((tm, HEAD_PAD), lambda i: (i, 0))]
    out_shape = [jax.ShapeDtypeStruct((t, q_w), q_dt),
                 jax.ShapeDtypeStruct((t, KV_LORA), F32),
                 jax.ShapeDtypeStruct((t, HEAD_PAD), F32)]
    if want_kv:
        out_specs += [pl.BlockSpec((tm, 2 * nq), lambda i: (i, 0)),
                      pl.BlockSpec((tm, nq), lambda i: (i, 0))]
        out_shape += [jax.ShapeDtypeStruct((t, 2 * nq), BF16),
                      jax.ShapeDtypeStruct((t, nq), BF16)]
    return pl.pallas_call(
        functools.partial(_mla_prep_kernel, want_kv=want_kv, hp=hp),
        grid=(t // tm,),
        in_specs=in_specs, out_specs=out_specs, out_shape=out_shape,
        compiler_params=_cparams("parallel"),
        name="mla_prep",
    )(*args)


def _mla_prompt_kernel(qi_ref, ki_ref, q_ref, k_ref, v_ref, o_ref, m_s, l_s, acc_s, *, tq, sub):
    t = pl.program_id(2)
    qi = qi_ref[t]
    ki = ki_ref[t]

    @pl.when(ki == 0)
    def _():
        m_s[...] = jnp.full(m_s.shape, -jnp.inf, F32)
        l_s[...] = jnp.zeros(l_s.shape, F32)
        acc_s[...] = jnp.zeros(acc_s.shape, F32)

    def step(diagonal):
        for j in range(2):
            base = 2 * j * HEAD_PAD
            for r in range(tq // sub):
                rows = slice(r * sub, (r + 1) * sub)
                n_k = (r + 1) * sub if diagonal else tq
                k_hi = k_ref[0:n_k, base:base + HEAD_PAD]
                k_lo = k_ref[0:n_k, base + HEAD_PAD:base + 2 * HEAD_PAD]
                vv = v_ref[0:n_k, :]
                q2 = q_ref[rows, base:base + 2 * HEAD_PAD]
                s = (lax.dot_general(q2, jnp.concatenate([k_hi, k_hi], axis=1), _NT,
                                     preferred_element_type=F32)
                     + lax.dot_general(q2[:, :HEAD_PAD], k_lo, _NT, preferred_element_type=F32))
                if diagonal:
                    row = r * sub + lax.broadcasted_iota(jnp.int32, (sub, n_k), 0)
                    col = lax.broadcasted_iota(jnp.int32, (sub, n_k), 1)
                    s = jnp.where(col <= row, s, -jnp.inf)
                m_prev = m_s[j, rows]
                m_new = jnp.maximum(m_prev, jnp.max(s, axis=-1, keepdims=True))
                alpha = jnp.exp(m_prev - m_new)
                p = jnp.exp(s - m_new)
                l_s[j, rows] = alpha * l_s[j, rows] + jnp.sum(p, axis=-1, keepdims=True)
                pv = jnp.dot(p.astype(BF16), vv, preferred_element_type=F32)
                acc_s[j, rows] = alpha * acc_s[j, rows] + (pv[:, :HEAD_PAD] + pv[:, HEAD_PAD:])
                m_s[j, rows] = m_new

    @pl.when(ki < qi)
    def _():
        step(False)

    @pl.when(ki == qi)
    def _():
        step(True)
        lane = lax.broadcasted_iota(jnp.int32, (tq, 2 * V_HEAD), 1)
        o_ref[...] = jnp.where(lane < V_HEAD, acc_s[0] / l_s[0], acc_s[1] / l_s[1])


def _mla_prompt(q2, k2, v2, *, nseq, seqlen):
    tq = _tile(seqlen, 1024)
    sub = _tile(tq, 256)
    nq = seqlen // tq
    qi_np = np.array([i for i in range(nq) for _ in range(i + 1)], np.int32)
    ki_np = np.array([j for i in range(nq) for j in range(i + 1)], np.int32)
    n_tri = int(qi_np.shape[0])
    pair_w = 4 * HEAD_PAD
    grid_spec = pltpu.PrefetchScalarGridSpec(
        num_scalar_prefetch=2,
        grid=(nseq, N_HEADS // 2, n_tri),
        in_specs=[pl.BlockSpec((tq, pair_w), lambda b, hp_, t, qi, ki: (b * nq + qi[t], hp_)),
                  pl.BlockSpec((tq, pair_w), lambda b, hp_, t, qi, ki: (b * nq + ki[t], hp_)),
                  pl.BlockSpec((tq, 2 * HEAD_PAD), lambda b, hp_, t, qi, ki: (b * nq + ki[t], hp_))],
        out_specs=pl.BlockSpec((tq, 2 * V_HEAD), lambda b, hp_, t, qi, ki: (b * nq + qi[t], hp_)),
        scratch_shapes=[pltpu.VMEM((2, tq, 1), F32), pltpu.VMEM((2, tq, 1), F32),
                        pltpu.VMEM((2, tq, 2 * V_HEAD), F32)])
    return pl.pallas_call(
        functools.partial(_mla_prompt_kernel, tq=tq, sub=sub),
        grid_spec=grid_spec,
        out_shape=jax.ShapeDtypeStruct((nseq * seqlen, N_HEADS * V_HEAD), F32),
        compiler_params=_cparams("parallel", "parallel", "arbitrary"),
        name="mla_prompt",
    )(jnp.asarray(qi_np), jnp.asarray(ki_np), q2, k2, v2)


def _mla_sample_kernel(pt_ref, q_ref, ckv_hbm, kpe_hbm, cn_ref, kn_ref, wkt_ref, wktp_ref, fold_ref, wv_ref,
                       o_ref, lhs_s, qr_s, m_s, l_s, ctx_s, cbuf, kbuf, sem, *, layer, pps, cpp, n_new):
    b = pl.program_id(0)
    s_idx = pl.program_id(1)
    n_steps = pl.num_programs(1)
    n_rows = N_HEADS * n_new
    g = b * n_steps + s_idx
    slot = g % 2

    def page_copies(bb, ss, sl):
        cps = []
        for i in range(pps):
            page = pt_ref[bb, ss * pps + i]
            cps.append(pltpu.make_async_copy(ckv_hbm.at[layer, page],
                                             cbuf.at[sl, pl.ds(i * PAGE, PAGE), :], sem.at[0, sl]))
            cps.append(pltpu.make_async_copy(kpe_hbm.at[layer, page],
                                             kbuf.at[sl, :, pl.ds(i * PAGE, PAGE)], sem.at[1, sl]))
        return cps

    @pl.when(g == 0)
    def _():
        for cp in page_copies(b, s_idx, slot):
            cp.start()

    @pl.when(g + 1 < pl.num_programs(0) * n_steps)
    def _():
        for cp in page_copies((g + 1) // n_steps, (g + 1) % n_steps, 1 - slot):
            cp.start()

    for cp in page_copies(b, s_idx, slot):
        cp.wait()

    @pl.when(s_idx == 0)
    def _():
        q = q_ref[...]
        lane_head = lax.broadcasted_iota(jnp.int32, q.shape, 1) // HEAD_PAD
        qblk = jnp.concatenate([jnp.where(lane_head == h, q, 0.0) for h in range(N_HEADS)], axis=0)
        qblk = qblk.astype(BF16)
        lhs_s[0:N_HEADS * QK_NOPE, :] = wkt_ref[...]
        lhs_s[N_HEADS * QK_NOPE:, :] = jnp.dot(qblk, wktp_ref[...], preferred_element_type=F32).astype(BF16)
        qr_s[...] = jnp.dot(qblk, fold_ref[...], preferred_element_type=F32).astype(BF16)
        m_s[...] = jnp.full(m_s.shape, -jnp.inf, F32)
        l_s[...] = jnp.zeros(l_s.shape, F32)
        ctx_s[...] = jnp.zeros(ctx_s.shape, F32)

    def scores(c, sr, ssr):
        cb = c.astype(BF16)
        big = lax.dot_general(lhs_s[...], cb, _NT, preferred_element_type=F32)
        pieces = []
        for h in range(N_HEADS):
            kn = big[h * QK_NOPE:(h + 1) * QK_NOPE]
            ss = jnp.sum(kn * kn, axis=0, keepdims=True) + ssr
            rk = lax.rsqrt(ss * (1.0 / QK_DIM) + EPS)
            rows = slice(N_HEADS * QK_NOPE + h * n_new, N_HEADS * QK_NOPE + (h + 1) * n_new)
            pieces.append((big[rows] + sr[h * n_new:(h + 1) * n_new]) * rk)
        return cb, jnp.concatenate(pieces, axis=0)

    def update(chunks):
        m_new = m_s[...]
        for _, sc in chunks:
            m_new = jnp.maximum(m_new, jnp.max(sc, axis=-1, keepdims=True))
        alpha = jnp.exp(m_s[...] - m_new)
        l_new = alpha * l_s[...]
        ctx = alpha * ctx_s[...]
        for cb, sc in chunks:
            p = jnp.exp(sc - m_new)
            l_new = l_new + jnp.sum(p, axis=-1, keepdims=True)
            ctx = ctx + jnp.dot(p.astype(BF16), cb, preferred_element_type=F32)
        l_s[...] = l_new
        ctx_s[...] = ctx
        m_s[...] = m_new

    chunks = []
    for j in range(0, pps, cpp):
        keys = pl.ds(j * PAGE, cpp * PAGE)
        kpe_t = kbuf[slot, :, keys]
        chunks.append(scores(cbuf[slot, keys, :],
                             jnp.dot(qr_s[...], kpe_t.astype(BF16), preferred_element_type=F32),
                             jnp.sum(kpe_t * kpe_t, axis=0, keepdims=True)))
    update(chunks)

    @pl.when(s_idx == pl.num_programs(1) - 1)
    def _():
        cpad = jnp.concatenate([cn_ref[...], jnp.zeros((PAGE - n_new, KV_LORA), F32)], axis=0)
        kpad = jnp.concatenate([kn_ref[...], jnp.zeros((PAGE - n_new, QK_ROPE), F32)], axis=0)
        key = lax.broadcasted_iota(jnp.int32, (n_rows, PAGE), 1)
        qry = lax.broadcasted_iota(jnp.int32, (n_rows, PAGE), 0) % n_new
        sr = lax.dot_general(qr_s[...], kpad.astype(BF16), _NT, preferred_element_type=F32)
        ones = jnp.ones((8, QK_ROPE), BF16)
        ssr = lax.dot_general(ones, (kpad * kpad).astype(BF16), _NT, preferred_element_type=F32)[0:1]
        cb, sc = scores(cpad, sr, ssr)
        update([(cb, jnp.where(key <= qry, sc, -jnp.inf))])
        ctx = (ctx_s[...] / l_s[...]).astype(BF16)
        res = jnp.dot(ctx, wv_ref[...], preferred_element_type=F32)
        lane_head = lax.broadcasted_iota(jnp.int32, (n_new, N_HEADS * V_HEAD), 1) // V_HEAD
        out = jnp.zeros((n_new, N_HEADS * V_HEAD), F32)
        for h in range(N_HEADS):
            out = out + jnp.where(lane_head == h, res[h * n_new:(h + 1) * n_new], 0.0)
        o_ref[...] = out


def _mla_sample(q, c_new, kpe_new, cache_ckv, cache_kpe_t, page_table, layer, lw):
    nseq, n_new, _ = q.shape
    n_pages = page_table.shape[1]
    pps = _tile(n_pages, 64)
    n_steps = n_pages // pps
    n_rows = N_HEADS * n_new

    in_specs = [pl.BlockSpec((None, n_new, N_HEADS * HEAD_PAD), lambda b, s, pt: (b, 0, 0)),
                pl.BlockSpec(memory_space=pl.ANY), pl.BlockSpec(memory_space=pl.ANY),
                pl.BlockSpec((None, n_new, KV_LORA), lambda b, s, pt: (b, 0, 0)),
                pl.BlockSpec((None, n_new, QK_ROPE), lambda b, s, pt: (b, 0, 0)),
                _const_spec(lw["wkt"].shape), _const_spec(lw["wktp"].shape),
                _const_spec(lw["fold"].shape), _const_spec(lw["wv"].shape)]
    grid_spec = pltpu.PrefetchScalarGridSpec(
        num_scalar_prefetch=1,
        grid=(nseq, n_steps),
        in_specs=in_specs,
        out_specs=pl.BlockSpec((None, n_new, N_HEADS * V_HEAD), lambda b, s, pt: (b, 0, 0)),
        scratch_shapes=[pltpu.VMEM((N_HEADS * QK_NOPE + n_rows, KV_LORA), BF16),
                        pltpu.VMEM((n_rows, QK_ROPE), BF16),
                        pltpu.VMEM((n_rows, 1), F32), pltpu.VMEM((n_rows, 1), F32),
                        pltpu.VMEM((n_rows, KV_LORA), F32),
                        pltpu.VMEM((2, pps * PAGE, KV_LORA), F32),
                        pltpu.VMEM((2, QK_ROPE, pps * PAGE), F32),
                        pltpu.SemaphoreType.DMA((2, 2))])
    return pl.pallas_call(
        functools.partial(_mla_sample_kernel, layer=layer, pps=pps, cpp=_tile(pps, 4), n_new=n_new),
        grid_spec=grid_spec,
        out_shape=jax.ShapeDtypeStruct((nseq, n_new, N_HEADS * V_HEAD), F32),
        compiler_params=_cparams("arbitrary", "arbitrary"),
        name="mla_sample",
    )(page_table, q, cache_ckv, cache_kpe_t, c_new, kpe_new,
      lw["wkt"], lw["wktp"], lw["fold"], lw["wv"])


def _memkv_kernel(x_ref, g_ref, wh_ref, wl_ref, gk_ref, k_ref, v_ref, *, hp):
    kv = _mm(_rms(x_ref[...], g_ref[...]), wh_ref[...], wl_ref[...], hp)
    for h in range(MEM_HEADS):
        sl = slice(h * MEM_HD, (h + 1) * MEM_HD)
        k_ref[:, sl] = _rms(kv[:, sl], gk_ref[...])
    v_ref[...] = kv[:, MEM_W:]


def _memkv(mem, g_mem, wmkv_hl, g_mkn, *, hp):
    t = mem.shape[0]
    tm = _tile(t, 256)
    return pl.pallas_call(
        functools.partial(_memkv_kernel, hp=hp),
        grid=(t // tm,),
        in_specs=[pl.BlockSpec((tm, D_MODEL), lambda i: (i, 0)),
                  _const_spec((1, D_MODEL)), _const_spec(wmkv_hl[0].shape), _const_spec(wmkv_hl[1].shape),
                  _const_spec((1, MEM_HD))],
        out_specs=[pl.BlockSpec((tm, MEM_W), lambda i: (i, 0))] * 2,
        out_shape=[jax.ShapeDtypeStruct((t, MEM_W), F32)] * 2,
        compiler_params=_cparams("parallel"),
        name="memkv",
    )(mem, g_mem, *wmkv_hl, g_mkn)


def _memattn_kernel(q_ref, k_ref, v_ref, g_ref, o_ref, *, hp, rows_by_head):
    scale = 1.0 / math.sqrt(MEM_HD)
    for h in range(MEM_HEADS):
        sl = slice(h * MEM_HD, (h + 1) * MEM_HD)
        if rows_by_head:
            k = k_ref[pl.ds(h, N_MEM, stride=MEM_HEADS), :]
            v = v_ref[pl.ds(h, N_MEM, stride=MEM_HEADS), :]
        else:
            k = k_ref[:, sl]
            v = v_ref[:, sl]
        q = _rms(q_ref[:, sl], g_ref[...]) * scale
        s = _mm2(q, k, hp, _NT)
        p = jnp.exp(s - jnp.max(s, axis=-1, keepdims=True))
        l = jnp.sum(p, axis=-1, keepdims=True)
        o_ref[:, sl] = _mm2(p, v, hp) / l


def _memattn(zr, mk, mv, g_mqn, *, nseq, seqlen, kv_index, hp):
    tq = _tile(seqlen, 512)
    nq = seqlen // tq
    lead = (None,) * (mk.ndim - 2)
    kv_blk = (N_MEM, MEM_W) if mk.shape[-1] == MEM_W else (N_MEM * MEM_HEADS, MEM_HD)
    return pl.pallas_call(
        functools.partial(_memattn_kernel, hp=hp, rows_by_head=kv_blk[1] == MEM_HD),
        grid=(nseq, nq),
        in_specs=[pl.BlockSpec((tq, ZR_MQ_BLK[0]), lambda b, i: (b * nq + i, ZR_MQ_BLK[1])),
                  pl.BlockSpec(lead + kv_blk, lambda b, i: kv_index(b)),
                  pl.BlockSpec(lead + kv_blk, lambda b, i: kv_index(b)),
                  _const_spec((1, MEM_HD))],
        out_specs=pl.BlockSpec((tq, MEM_W), lambda b, i: (b * nq + i, 0)),
        out_shape=jax.ShapeDtypeStruct((nseq * seqlen, MEM_W), F32),
        compiler_params=_cparams("parallel", "parallel"),
        name="memattn",
    )(zr, mk, mv, g_mqn)


def _merge_kernel(x_ref, ya_ref, yb_ref, yc_ref, ym_ref, zg_ref, wbrh_ref, wbrl_ref, woh_ref, wol_ref,
                  gf_ref, wrh_ref, wrl_ref, br_ref, x1_ref, h2_ref, cw_ref, *, hp):
    acc = None
    for b, y_ref in enumerate((ya_ref, yb_ref, yc_ref, ym_ref)):
        term = zg_ref[:, b * D_MODEL:(b + 1) * D_MODEL] * _mm(y_ref[...], wbrh_ref[b], wbrl_ref[b], hp)
        acc = term if acc is None else acc + term
    x1 = x_ref[...] + _mm(acc, woh_ref[...], wol_ref[...], hp)
    x1_ref[...] = x1
    h2 = _rms(x1, gf_ref[...])
    h2_ref[...] = h2
    logits = _mm(h2, wrh_ref[...], wrl_ref[...], True) + br_ref[...]
    lane = lax.broadcasted_iota(jnp.int32, logits.shape, 1).astype(F32)
    big = 1e9
    is_grp = (lane >= N_EXPERTS) & (lane < N_EXPERTS + N_GROUPS)
    gl = jnp.where(is_grp, logits, -jnp.inf)
    ge = jnp.exp(gl - jnp.max(gl, axis=-1, keepdims=True))
    gprob = ge / jnp.sum(ge, axis=-1, keepdims=True)
    gp = jnp.max(gprob, axis=-1, keepdims=True)
    gi = jnp.min(jnp.where(is_grp & (gprob == gp), lane, big), axis=-1, keepdims=True) - N_EXPERTS
    sel = (lane >= gi * EXP_PER_GROUP) & (lane < (gi + 1) * EXP_PER_GROUP)
    el = jnp.where(sel, logits, -jnp.inf)
    ee = jnp.exp(el - jnp.max(el, axis=-1, keepdims=True))
    eprob = ee / jnp.sum(ee, axis=-1, keepdims=True)
    v1 = jnp.max(eprob, axis=-1, keepdims=True)
    i1 = jnp.min(jnp.where(sel & (eprob == v1), lane, big), axis=-1, keepdims=True)
    rest = jnp.where(sel & (lane != i1), eprob, -1.0)
    v2 = jnp.max(rest, axis=-1, keepdims=True)
    i2 = jnp.min(jnp.where(rest == v2, lane, big), axis=-1, keepdims=True)
    den = v1 + v2
    cw = jnp.where(lane == i1, gp * v1 / den, jnp.where(lane == i2, gp * v2 / den, 0.0))
    cw_ref[...] = jnp.where(lane == GROUP_LANE, gi, cw)


def _merge(x, ya, yb, yc, ym, zg, lw, *, hp):
    t = x.shape[0]
    tm = _tile(t, 256)
    row = lambda w: pl.BlockSpec((tm, w), lambda i: (i, 0))
    consts = [*lw["wbr"], *lw["wo"], lw["g_ffn"], *lw["wr"], lw["b_r"]]
    return pl.pallas_call(
        functools.partial(_merge_kernel, hp=hp),
        grid=(t // tm,),
        in_specs=[row(D_MODEL), row(POOL_W), row(512), row(SG_W), row(MEM_W), row(N_BRANCH * D_MODEL)]
                 + [_const_spec(c.shape) for c in consts],
        out_specs=[row(D_MODEL), row(D_MODEL), row(128)],
        out_shape=[jax.ShapeDtypeStruct((t, D_MODEL), F32),
                   jax.ShapeDtypeStruct((t, D_MODEL), F32),
                   jax.ShapeDtypeStruct((t, 128), F32)],
        compiler_params=_cparams("parallel"),
        name="merge",
    )(x, ya, yb, yc, ym, zg, *consts)


def _moe_kernel(h_ref, cw_ref, x_ref, wguh_ref, wgul_ref, wdh_ref, wdl_ref, o_ref, acc_s, hh_s, hl_s, *, hp):
    e = pl.program_id(1)

    @pl.when(e == 0)
    def _():
        acc_s[...] = jnp.zeros(acc_s.shape, F32)
        hh_s[...], hl_s[...] = _split(h_ref[...])

    dot = lambda a, b: jnp.dot(a, b, preferred_element_type=F32)
    ab = dot(hh_s[...], wguh_ref[...])
    if hp:
        ab = ab + (dot(hl_s[...], wguh_ref[...]) + dot(hh_s[...], wgul_ref[...]))
    cw = cw_ref[...]
    lane = lax.broadcasted_iota(jnp.int32, cw.shape, 1)
    cwe = jnp.sum(jnp.where(lane == e, cw, 0.0), axis=-1, keepdims=True)
    t = jax.nn.silu(ab[:, :D_EXPERT]) * ab[:, D_EXPERT:] * cwe
    acc_s[...] += _mm(t, wdh_ref[...], wdl_ref[...], hp)

    @pl.when(e == N_EXPERTS - 1)
    def _():
        o_ref[...] = x_ref[...] + acc_s[...]


def _moe(h2, cw, x1, wgu_hl, wd_hl, *, hp):
    t = h2.shape[0]
    tm = _tile(t, 512)
    wgu_spec = pl.BlockSpec((None, D_MODEL, 2 * D_EXPERT), lambda i, e: (e, 0, 0))
    wd_spec = pl.BlockSpec((None, D_EXPERT, D_MODEL), lambda i, e: (e, 0, 0))
    return pl.pallas_call(
        functools.partial(_moe_kernel, hp=hp),
        grid=(t // tm, N_EXPERTS),
        in_specs=[pl.BlockSpec((tm, D_MODEL), lambda i, e: (i, 0)),
                  pl.BlockSpec((tm, 128), lambda i, e: (i, 0)),
                  pl.BlockSpec((tm, D_MODEL), lambda i, e: (i, 0)),
                  wgu_spec, wgu_spec, wd_spec, wd_spec],
        out_specs=pl.BlockSpec((tm, D_MODEL), lambda i, e: (i, 0)),
        out_shape=jax.ShapeDtypeStruct((t, D_MODEL), F32),
        scratch_shapes=[pltpu.VMEM((tm, D_MODEL), F32), pltpu.VMEM((tm, D_MODEL), BF16),
                        pltpu.VMEM((tm, D_MODEL), BF16)],
        compiler_params=_cparams("parallel", "arbitrary"),
        name="moe",
    )(h2, cw, x1, *wgu_hl, *wd_hl)


def _split3(a):
    mask = lambda v: lax.bitcast_convert_type(lax.bitcast_convert_type(v, jnp.int32) & jnp.int32(-65536), F32)
    p1 = mask(a)
    r1 = a - p1
    p2 = mask(r1)
    return p1.astype(BF16), p2.astype(BF16), (r1 - p2).astype(BF16)


def _moe_sorted_kernel(h_ref, cw_ref, x_ref, wguh_ref, wgul_ref, wdh_ref, wdl_ref, o_ref,
                       xh_s, xl_s, cws_s, ys_s, pt_s, seg_s, *, tb, ch):
    e = pl.program_id(1)
    dot = lambda a, b: jnp.dot(a, b, preferred_element_type=F32)

    @pl.when(e == 0)
    def _():
        cw = cw_ref[...]
        lane = lax.broadcasted_iota(jnp.int32, (tb, 128), 1)
        gi = jnp.sum(jnp.where(lane == GROUP_LANE, cw, 0.0), axis=-1, keepdims=True)
        oh = jnp.where(lane.astype(F32) == gi, 1.0, 0.0)
        r = lax.broadcasted_iota(jnp.int32, (tb, tb), 0)
        c = lax.broadcasted_iota(jnp.int32, (tb, tb), 1)
        lower = jnp.where(c < r, 1.0, 0.0).astype(BF16)
        rank = dot(lower, oh.astype(BF16))
        cnt = jnp.sum(oh, axis=0, keepdims=True)
        lane_row = lax.broadcasted_iota(jnp.int32, (1, 128), 1)
        off = jnp.zeros((1, 128), F32)
        for g in range(N_GROUPS - 1):
            off = off + jnp.where(lane_row > g, cnt[:, g:g + 1], 0.0)
        slot = jnp.sum(oh * (off + rank), axis=-1, keepdims=True)
        pt_s[...] = jnp.where(slot == c.astype(F32), 1.0, 0.0).astype(BF16)
        d_hi = jnp.floor(slot * (1.0 / 32.0))
        digits = jnp.where(lane == 0, d_hi, jnp.where(lane == 1, slot - 32.0 * d_hi, 0.0)).astype(BF16)
        er = lax.broadcasted_iota(jnp.int32, (128, 128), 0)
        ec = lax.broadcasted_iota(jnp.int32, (128, 128), 1)
        eye = jnp.where(er == ec, 1.0, 0.0).astype(BF16)
        dt = lax.dot_general(eye, digits, _NT, preferred_element_type=F32)
        slot_row = 32.0 * dt[0:1] + dt[1:2]
        p = jnp.where(slot_row == r.astype(F32), 1.0, 0.0).astype(BF16)
        h_hi, h_lo = _split(h_ref[...])
        xh_s[...] = dot(p, h_hi).astype(BF16)
        xl_s[...] = dot(p, h_lo).astype(BF16)
        c1, c2, c3 = _split3(cw)
        cws_s[...] = dot(p, c1) + (dot(p, c2) + dot(p, c3))
        ys_s[...] = jnp.zeros(ys_s.shape, F32)
        for g in range(N_GROUPS):
            seg_s[g] = off[0, g].astype(jnp.int32)
            seg_s[N_GROUPS + g] = (off[0, g] + cnt[0, g]).astype(jnp.int32)

    grp = e // EXP_PER_GROUP
    start = seg_s[grp]
    end = seg_s[N_GROUPS + grp]
    w0 = (start // 16) * 16
    n_win = jnp.where(end > start, (end - w0 + ch - 1) // ch, 0)

    def window(j, carry):
        lo = w0 + j * ch
        ws = pl.multiple_of(jnp.minimum(lo, tb - ch), 16)
        rows = pl.ds(ws, ch)
        xh = xh_s[rows, :]
        ab = dot(xh, wguh_ref[...]) + (dot(xl_s[rows, :], wguh_ref[...]) + dot(xh, wgul_ref[...]))
        cws = cws_s[rows, :]
        lane = lax.broadcasted_iota(jnp.int32, cws.shape, 1)
        row = ws + lax.broadcasted_iota(jnp.int32, (ch, 1), 0)
        cwe = jnp.sum(jnp.where(lane == e, cws, 0.0), axis=-1, keepdims=True)
        cwe = jnp.where(row >= lo, cwe, 0.0)
        t = jax.nn.silu(ab[:, :D_EXPERT]) * ab[:, D_EXPERT:] * cwe
        ys_s[rows, :] += _mm(t, wdh_ref[...], wdl_ref[...], True)
        return carry

    lax.fori_loop(0, n_win, window, 0)

    @pl.when(e == N_EXPERTS - 1)
    def _():
        y_hi, y_lo = _split(ys_s[...])
        pt = pt_s[...]
        o_ref[...] = x_ref[...] + (dot(pt, y_hi) + dot(pt, y_lo))


def _moe_sorted(h2, cw, x1, wgu_hl, wd_hl):
    t = h2.shape[0]
    tb = 1024
    once = lambda w: pl.BlockSpec((tb, w), lambda i, e: (i, 0), pipeline_mode=pl.Buffered(1))
    wgu_spec = pl.BlockSpec((None, D_MODEL, 2 * D_EXPERT), lambda i, e: (e, 0, 0))
    wd_spec = pl.BlockSpec((None, D_EXPERT, D_MODEL), lambda i, e: (e, 0, 0))
    return pl.pallas_call(
        functools.partial(_moe_sorted_kernel, tb=tb, ch=320),
        grid=(t // tb, N_EXPERTS),
        in_specs=[once(D_MODEL), once(128), once(D_MODEL), wgu_spec, wgu_spec, wd_spec, wd_spec],
        out_specs=pl.BlockSpec((tb, D_MODEL), lambda i, e: (i, 0)),
        out_shape=jax.ShapeDtypeStruct((t, D_MODEL), F32),
        scratch_shapes=[pltpu.VMEM((tb, D_MODEL), BF16), pltpu.VMEM((tb, D_MODEL), BF16),
                        pltpu.VMEM((tb, 128), F32), pltpu.VMEM((tb, D_MODEL), F32),
                        pltpu.VMEM((tb, tb), BF16), pltpu.SMEM((2 * N_GROUPS,), jnp.int32)],
        compiler_params=_cparams("parallel", "arbitrary"),
        name="moe_sorted",
    )(h2, cw, x1, *wgu_hl, *wd_hl)


def _hl(w):
    b = lax.bitcast_convert_type(w, jnp.int32)
    b = (b + jnp.int32(0x7FFF) + ((b >> 16) & 1)) & jnp.int32(-65536)
    hi = lax.bitcast_convert_type(b, F32)
    return hi.astype(BF16), (w - hi).astype(BF16)


def _rope_tables(pos):
    half = QK_ROPE // 2
    inv = ROPE_THETA ** (-jnp.arange(half, dtype=F32) / half)
    ang = pos.astype(F32)[:, None] * inv
    cos, sin = jnp.cos(ang), jnp.sin(ang)
    n = pos.shape[0]
    cos128 = jnp.concatenate([jnp.ones((n, QK_NOPE), F32), cos, cos, jnp.zeros((n, 32), F32)], 1)
    sin128 = jnp.concatenate([jnp.zeros((n, QK_NOPE), F32), sin, sin, jnp.zeros((n, 32), F32)], 1)
    return cos128, sin128


def _fold_matrix():
    f = np.zeros((N_HEADS * HEAD_PAD, QK_ROPE), np.float32)
    for h in range(N_HEADS):
        for j in range(QK_ROPE):
            f[h * HEAD_PAD + QK_NOPE + j, j] = 1.0
    return jnp.asarray(f, BF16)


def _head_pad_gain(g):
    return jnp.tile(jnp.concatenate([g, jnp.zeros((HEAD_PAD - QK_DIM,), F32)]), N_HEADS)[None]


def _layer_weights(l, w):
    half = QK_ROPE // 2
    wi = w["w_in"][l]
    w_a, w_qa = wi[:, 0:512], wi[:, 512:768]
    w_lat, w_rope = wi[:, 768:896], wi[:, 896:928]
    w_c, w_mq, w_g = wi[:, 928:1952], wi[:, 1952:2464], wi[:, 2464:]
    z64 = jnp.zeros((D_MODEL, QK_NOPE), F32)
    z32 = jnp.zeros((D_MODEL, 32), F32)
    rope128 = jnp.concatenate([z64, w_rope, z32], 1)
    ropesw128 = jnp.concatenate([z64, -w_rope[:, half:], w_rope[:, :half], z32], 1)
    lw = {}
    lw["wr_in"] = _hl(jnp.concatenate([w_c, w_a, w_mq, w_qa, w_lat, rope128, ropesw128], 1))
    lw["wg_in"] = _hl(w_g)
    lw["g_mix"] = w["g_mix"][l][None]

    wq = w["w_qb"][l].reshape(Q_LORA, N_HEADS, QK_DIM)
    nope, x1, x2 = wq[..., :QK_NOPE], wq[..., QK_NOPE:QK_NOPE + half], wq[..., QK_NOPE + half:]
    zq = lambda n: jnp.zeros((Q_LORA, N_HEADS, n), F32)
    wq1 = jnp.concatenate([nope, x1, x2, zq(32)], -1).reshape(Q_LORA, N_HEADS * HEAD_PAD)
    wq2 = jnp.concatenate([zq(QK_NOPE), -x2, x1, zq(32)], -1).reshape(Q_LORA, N_HEADS * HEAD_PAD)
    lw["wq"] = _hl(jnp.concatenate([wq1, wq2], 1))
    lw["g_qa"] = w["g_qa"][l][None]
    lw["g_kva"] = w["g_kva"][l][None]
    lw["g_qn"] = _head_pad_gain(w["g_qn"][l])
    lw["g_kn"] = _head_pad_gain(w["g_kn"][l])

    wkv = w["w_kvb"][l].reshape(KV_LORA, N_HEADS, QK_NOPE + V_HEAD)
    kn, vv = wkv[..., :QK_NOPE], wkv[..., QK_NOPE:]
    wk_pad = jnp.concatenate([kn, jnp.zeros((KV_LORA, N_HEADS, HEAD_PAD - QK_NOPE), F32)], -1)
    wk_pad = wk_pad.reshape(KV_LORA, N_HEADS * HEAD_PAD)
    wv = vv.reshape(KV_LORA, N_HEADS * V_HEAD)
    lw["wkv"] = _hl(jnp.concatenate([wk_pad, wv], 1))
    lw["wkt"] = kn.reshape(KV_LORA, N_HEADS * QK_NOPE).T.astype(BF16)
    lw["wktp"] = wk_pad.T.astype(BF16)
    lw["wv"] = wv.astype(BF16)
    lw["fold"] = _fold_matrix()

    lw["w_pool"] = _hl(w["w_pool"][l])
    lw["pool_scale"] = w["pool_scale"][l][None]
    lw["g_sgv"] = w["g_sgv"][l]
    lw["w_sp"] = w["w_sp"][l]
    lw["b_sp"] = w["b_sp"][l]
    lw["g_mem"] = w["g_mem"][l][None]
    lw["wmkv"] = _hl(jnp.concatenate([w["w_mk"][l], w["w_mv"][l]], 1))
    lw["g_mqn"] = w["g_mqn"][l][None]
    lw["g_mkn"] = w["g_mkn"][l][None]
    lw["wbr"] = _hl(w["w_br"][l])
    lw["wo"] = _hl(w["w_o"][l])
    lw["g_ffn"] = w["g_ffn"][l][None]
    lw["wr"] = _hl(jnp.concatenate([w["w_re"][l], w["w_rg"][l],
                                    jnp.zeros((D_MODEL, 128 - N_EXPERTS - N_GROUPS), F32)], 1))
    lw["b_r"] = jnp.concatenate([w["b_re"][l], w["b_rg"][l],
                                 jnp.zeros((128 - N_EXPERTS - N_GROUPS,), F32)])[None]
    lw["wgu"] = _hl(jnp.concatenate([w["w_eg"][l], w["w_eu"][l]], -1))
    lw["wd"] = _hl(w["w_ed"][l])
    return lw


def _sgu_operands(lw, seqlen):
    blk = min(seqlen, CHUNK)
    rep = CHUNK // blk
    ws = jnp.tile(lw["w_sp"][:, :blk, :blk], (1, rep, rep))
    b = jnp.tile(lw["b_sp"][:, :blk], (1, rep))
    bfull = jnp.repeat(b.T, CHUNK, axis=1)
    return ws, bfull, blk


def _group_layer(x, lw, *, nseq, seqlen, pos0, cos, sin, pool_buf, mem_k, mem_v, kv_index,
                 mla_sample_fn, hp):
    zr = _inproj(x, lw["g_mix"], lw["wr_in"], hp=hp, sigmoid=False)
    zg = _inproj(x, lw["g_mix"], lw["wg_in"], hp=hp, sigmoid=True)
    ya, pool_new = _pool(zr, pool_buf, lw["w_pool"], lw["pool_scale"],
                         nseq=nseq, seqlen=seqlen, pos0=pos0, hp=hp)
    ws, bfull, blk = _sgu_operands(lw, seqlen)
    yc, v_rows = _sgu(zr, lw["g_sgv"], ws, bfull, blk=blk, hp=hp)
    scale = 1.0 / math.sqrt(QK_DIM)
    if mla_sample_fn is not None:
        q, c, kpe128 = _mla_prep(zr, cos, sin, lw, lw["g_kn"] * scale, want_kv=False, hp=hp)
        kpe = kpe128[:, QK_NOPE:QK_DIM]
        yb = mla_sample_fn(q, c, kpe)
    else:
        qpost = jnp.full((1, N_HEADS * HEAD_PAD), scale, F32)
        q, c, kpe128, k, v = _mla_prep(zr, cos, sin, lw, qpost, want_kv=True, hp=hp)
        kpe = kpe128[:, QK_NOPE:QK_DIM]
        yb = _mla_prompt(q, k, v, nseq=nseq, seqlen=seqlen)
    ym = _memattn(zr, mem_k, mem_v, lw["g_mqn"], nseq=nseq, seqlen=seqlen, kv_index=kv_index, hp=hp)
    x1, h2, cw = _merge(x, ya, yb, yc, ym, zg, lw, hp=hp)
    if hp and x.shape[0] % 1024 == 0:
        x2 = _moe_sorted(h2, cw, x1, lw["wgu"], lw["wd"])
    else:
        x2 = _moe(h2, cw, x1, lw["wgu"], lw["wd"], hp=hp)
    return x2, c, kpe, pool_new, v_rows


def kernel(x_prompt, x_sample, mem_prompt, cache_ckv, cache_kpe, cache_memk, cache_memv, state_pool, page_table, g_mix, w_in, g_qa, w_qb, g_kva, w_kvb, g_qn, g_kn, w_pool, pool_scale, g_sgv, w_sp, b_sp, g_mem, w_mk, w_mv, g_mqn, g_mkn, w_br, w_o, g_ffn, w_rg, b_rg, w_re, b_re, w_eg, w_eu, w_ed):
    w = dict(g_mix=g_mix, w_in=w_in, g_qa=g_qa, w_qb=w_qb, g_kva=g_kva, w_kvb=w_kvb, g_qn=g_qn, g_kn=g_kn,
             w_pool=w_pool, pool_scale=pool_scale, g_sgv=g_sgv, w_sp=w_sp, b_sp=b_sp, g_mem=g_mem,
             w_mk=w_mk, w_mv=w_mv, g_mqn=g_mqn, g_mkn=g_mkn, w_br=w_br, w_o=w_o, g_ffn=g_ffn,
             w_rg=w_rg, b_rg=b_rg, w_re=w_re, b_re=b_re, w_eg=w_eg, w_eu=w_eu, w_ed=w_ed)
    nb, seq, _ = x_prompt.shape
    ndb, t_new, _ = x_sample.shape
    depth = w_in.shape[0]
    past = page_table.shape[1] * PAGE

    cos_p, sin_p = _rope_tables(jnp.arange(seq, dtype=jnp.int32))
    cos_s, sin_s = _rope_tables(past + jnp.arange(t_new, dtype=jnp.int32))
    tm_s = _tile(ndb * t_new, 512)
    cos_s = jnp.tile(cos_s, (tm_s // t_new, 1))
    sin_s = jnp.tile(sin_s, (tm_s // t_new, 1))

    xp = x_prompt.reshape(nb * seq, D_MODEL)
    xs = x_sample.reshape(ndb * t_new, D_MODEL)
    mem = mem_prompt.reshape(nb * N_MEM, D_MODEL)
    zero_buf = jnp.zeros((nb, POOL_PAD, POOL_W), F32)
    memk_s = cache_memk.reshape(depth, ndb, N_MEM * MEM_HEADS, MEM_HD)
    memv_s = cache_memv.reshape(depth, ndb, N_MEM * MEM_HEADS, MEM_HD)
    cache_kpe_t = jnp.swapaxes(cache_kpe, 2, 3)

    outs = [[] for _ in range(9)]
    for l in range(depth):
        lw = _layer_weights(l, w)
        mk, mv = _memkv(mem, lw["g_mem"], lw["wmkv"], lw["g_mkn"], hp=True)
        xp, c, kp, pb, _ = _group_layer(
            xp, lw, nseq=nb, seqlen=seq, pos0=0, cos=cos_p, sin=sin_p, pool_buf=zero_buf,
            mem_k=mk, mem_v=mv, kv_index=lambda b: (b, 0), mla_sample_fn=None, hp=True)
        outs[0].append(c.reshape(nb, seq, KV_LORA))
        outs[1].append(kp.reshape(nb, seq, QK_ROPE))
        outs[2].append(mk.reshape(nb, N_MEM, MEM_HEADS, MEM_HD))
        outs[3].append(mv.reshape(nb, N_MEM, MEM_HEADS, MEM_HD))
        outs[4].append(pb[:, 1:])

        def mla_sample_fn(q, c_new, kpe_new, l=l, lw=lw):
            o = _mla_sample(q.reshape(ndb, t_new, -1), c_new.reshape(ndb, t_new, KV_LORA),
                            kpe_new.reshape(ndb, t_new, QK_ROPE), cache_ckv, cache_kpe_t, page_table, l, lw)
            return o.reshape(ndb * t_new, N_HEADS * V_HEAD)

        buf_s = jnp.pad(state_pool[l], ((0, 0), (1, 0), (0, 0)))
        xs, c, kp, pb, vr = _group_layer(
            xs, lw, nseq=ndb, seqlen=t_new, pos0=past, cos=cos_s, sin=sin_s, pool_buf=buf_s,
            mem_k=memk_s, mem_v=memv_s, kv_index=lambda b, l=l: (l, b, 0, 0),
            mla_sample_fn=mla_sample_fn, hp=False)
        outs[5].append(c.reshape(ndb, t_new, KV_LORA))
        outs[6].append(kp.reshape(ndb, t_new, QK_ROPE))
        outs[7].append(pb[:, 1:])
        outs[8].append(vr.reshape(ndb, t_new, SG_W))

    stacked = [jnp.stack(o) for o in outs]
    return (xp.reshape(nb, seq, D_MODEL), xs.reshape(ndb, t_new, D_MODEL), *stacked)
```

```python
import functools
import math

import numpy as np
import jax
import jax.numpy as jnp
from jax import lax
from jax.experimental import pallas as pl
from jax.experimental.pallas import tpu as pltpu

F32 = jnp.float32
BF16 = jnp.bfloat16
EPS = 1e-6

D_MODEL = 1024
PAGE = 128
N_MEM = 256
POOL_WINDOWS = (2, 4, 8, 16)
POOL_GW = 128
POOL_W = 512
POOL_BUF = 15
POOL_PAD = 16
N_HEADS = 8
QK_NOPE = 64
QK_ROPE = 32
QK_DIM = QK_NOPE + QK_ROPE
V_HEAD = 64
HEAD_PAD = 128
Q_LORA = 256
KV_LORA = 128
ROPE_THETA = 10000.0
CHUNK = 128
SG_GROUPS = 4
SG_W = 512
MEM_HEADS = 4
MEM_HD = 128
MEM_W = 512
N_BRANCH = 4
N_GROUPS = 4
EXP_PER_GROUP = 4
N_EXPERTS = 16
D_EXPERT = 256
GROUP_LANE = N_EXPERTS

ZR_W = 2688
ZR_C_BLK = (1024, 0)
ZR_A_BLK = (512, 2)
ZR_MQ_BLK = (512, 3)
ZR_QA_BLK = (256, 8)
ZR_KV_BLK = (384, 6)

VMEM_LIMIT = 56 * 1024 * 1024

_NN = (((1,), (0,)), ((), ()))
_NT = (((1,), (1,)), ((), ()))


def _tile(n, pref):
    return pref if n % pref == 0 else n


def _cparams(*sem):
    return pltpu.CompilerParams(dimension_semantics=sem, vmem_limit_bytes=VMEM_LIMIT)


def _rms(x, g):
    return x * lax.rsqrt(jnp.mean(x * x, axis=-1, keepdims=True) + EPS) * g


def _const_spec(shape):
    nd = len(shape)
    return pl.BlockSpec(shape, lambda *_: (0,) * nd, pipeline_mode=pl.Buffered(1))


def _split(a):
    hi = lax.bitcast_convert_type(lax.bitcast_convert_type(a, jnp.int32) & jnp.int32(-65536), F32)
    return hi.astype(BF16), (a - hi).astype(BF16)


def _mm(a, w_hi, w_lo, hp, dims=_NN):
    dot = lambda x, y: lax.dot_general(x, y, dims, preferred_element_type=F32)
    if not hp:
        return dot(a.astype(BF16), w_hi)
    a_hi, a_lo = _split(a)
    return dot(a_hi, w_hi) + (dot(a_lo, w_hi) + dot(a_hi, w_lo))


def _mm2(a, b, hp, dims=_NN):
    if not hp:
        return lax.dot_general(a.astype(BF16), b.astype(BF16), dims, preferred_element_type=F32)
    b_hi, b_lo = _split(b)
    return _mm(a, b_hi, b_lo, True, dims)


def _inproj_kernel(x_ref, g_ref, wh_ref, wl_ref, z_ref, *, hp, sigmoid):
    h = _rms(x_ref[...], g_ref[...])
    z = _mm(h, wh_ref[...], wl_ref[...], hp)
    z_ref[...] = jax.nn.sigmoid(z) if sigmoid else z


def _inproj(x, g, w_hl, *, hp, sigmoid):
    t = x.shape[0]
    tm = _tile(t, 256)
    n = w_hl[0].shape[1]
    return pl.pallas_call(
        functools.partial(_inproj_kernel, hp=hp, sigmoid=sigmoid),
        grid=(t // tm,),
        in_specs=[pl.BlockSpec((tm, D_MODEL), lambda i: (i, 0)),
                  _const_spec((1, D_MODEL)),
                  _const_spec(w_hl[0].shape), _const_spec(w_hl[1].shape)],
        out_specs=pl.BlockSpec((tm, n), lambda i: (i, 0)),
        out_shape=jax.ShapeDtypeStruct((t, n), F32),
        compiler_params=_cparams("parallel"),
        name="inproj",
    )(x, g, *w_hl)


def _pool_kernel(a_ref, buf_ref, wph_ref, wpl_ref, ps_ref, y_ref, pn_ref, ext_ref, *, nb, ts, pos0, hp):
    s = pl.program_id(1)

    @pl.when(s == 0)
    def _():
        ext_ref[:, 0:POOL_PAD, :] = buf_ref[...]

    a = a_ref[...].reshape(nb, ts, POOL_W)
    ext_ref[:, POOL_PAD:POOL_PAD + ts, :] = a
    pos = pos0 + s * ts + lax.broadcasted_iota(jnp.int32, (1, ts, 1), 1)
    for g, w in enumerate(POOL_WINDOWS):
        sl = slice(g * POOL_GW, (g + 1) * POOL_GW)
        acc = a[:, :, sl]
        for k in range(1, w):
            acc = acc + ext_ref[:, POOL_PAD - k:POOL_PAD - k + ts, sl]
        cnt = jnp.minimum(pos + 1, w).astype(F32)
        m = (acc / cnt - a[:, :, sl]).reshape(nb * ts, POOL_GW)
        y_ref[:, sl] = _mm(m, wph_ref[g], wpl_ref[g], hp) * ps_ref[:, sl]
    tail = ext_ref[:, ts:ts + POOL_PAD, :]
    pn_ref[...] = tail
    ext_ref[:, 0:POOL_PAD, :] = tail


def _pool(zr, buf, wp_hl, ps, *, nseq, seqlen, pos0, hp):
    if seqlen >= 512:
        nb, ts = 1, 512
    else:
        nb, ts = _tile(nseq, 16), seqlen
    n_s = seqlen // ts
    kern = functools.partial(_pool_kernel, nb=nb, ts=ts, pos0=pos0, hp=hp)
    return pl.pallas_call(
        kern,
        grid=(nseq // nb, n_s),
        in_specs=[pl.BlockSpec((nb * ts, POOL_W), lambda b, s: (b * n_s + s, ZR_A_BLK[1])),
                  pl.BlockSpec((nb, POOL_PAD, POOL_W), lambda b, s: (b, 0, 0)),
                  _const_spec(wp_hl[0].shape), _const_spec(wp_hl[1].shape),
                  _const_spec((1, POOL_W))],
        out_specs=[pl.BlockSpec((nb * ts, POOL_W), lambda b, s: (b * n_s + s, 0)),
                   pl.BlockSpec((nb, POOL_PAD, POOL_W), lambda b, s: (b, 0, 0))],
        out_shape=[jax.ShapeDtypeStruct((nseq * seqlen, POOL_W), F32),
                   jax.ShapeDtypeStruct((nseq, POOL_PAD, POOL_W), F32)],
        scratch_shapes=[pltpu.VMEM((nb, POOL_PAD + ts, POOL_W), F32)],
        compiler_params=_cparams("parallel", "arbitrary"),
        name="pool",
    )(zr, buf, *wp_hl, ps)


def _sgu_kernel(zc_ref, g_ref, ws_ref, b_ref, y_ref, v_ref, *, n_chunk, blk, hp):
    z = jax.nn.gelu(zc_ref[...])
    r = lax.broadcasted_iota(jnp.int32, (CHUNK, CHUNK), 0)
    c = lax.broadcasted_iota(jnp.int32, (CHUNK, CHUNK), 1)
    mask = (c <= r) & ((r // blk) == (c // blk))
    for g in range(SG_GROUPS):
        sl = slice(g * CHUNK, (g + 1) * CHUNK)
        vg = _rms(z[:, SG_W + g * CHUNK:SG_W + (g + 1) * CHUNK], g_ref[g:g + 1, :])
        v_ref[:, sl] = vg
        ws = jnp.where(mask, ws_ref[g], 0.0)
        for ch in range(n_chunk):
            rows = slice(ch * CHUNK, (ch + 1) * CHUNK)
            sp = _mm2(ws, vg[rows], hp) + b_ref[:, sl]
            y_ref[rows, sl] = z[rows, sl] * sp


def _sgu(zr, g_sgv, ws, bfull, *, blk, hp):
    t = zr.shape[0]
    rows = _tile(t, 512)
    n_chunk = rows // CHUNK
    return pl.pallas_call(
        functools.partial(_sgu_kernel, n_chunk=n_chunk, blk=blk, hp=hp),
        grid=(t // rows,),
        in_specs=[pl.BlockSpec((rows, ZR_C_BLK[0]), lambda i: (i, ZR_C_BLK[1])),
                  _const_spec(g_sgv.shape),
                  _const_spec(ws.shape),
                  _const_spec(bfull.shape)],
        out_specs=[pl.BlockSpec((rows, SG_W), lambda i: (i, 0))] * 2,
        out_shape=[jax.ShapeDtypeStruct((t, SG_W), F32)] * 2,
        compiler_params=_cparams("parallel"),
        name="sgu",
    )(zr, g_sgv, ws, bfull)


def _mla_prep_kernel(qa_ref, kv_ref, cos_ref, sin_ref, gqa_ref, gkva_ref, wqh_ref, wql_ref, gqn_ref,
                     qpost_ref, wkvh_ref, wkvl_ref, gkn_ref, q_ref, c_ref, kpe_ref, *maybe_kv,
                     want_kv, hp):
    cos = cos_ref[...]
    sin = sin_ref[...]
    qn = _rms(qa_ref[...], gqa_ref[...])
    qq = _mm(qn, wqh_ref[...], wql_ref[...], hp)
    kv = kv_ref[...]
    c = _rms(kv[:, :KV_LORA], gkva_ref[...])
    c_ref[...] = c
    kpe = kv[:, KV_LORA:2 * KV_LORA] * cos + kv[:, 2 * KV_LORA:] * sin
    kpe_ref[...] = kpe
    nq = N_HEADS * HEAD_PAD

    def put_split(ref, idx, val):
        hi, lo = _split(val)
        ref[:, 2 * idx * HEAD_PAD:(2 * idx + 1) * HEAD_PAD] = hi
        ref[:, (2 * idx + 1) * HEAD_PAD:(2 * idx + 2) * HEAD_PAD] = lo

    if want_kv:
        k_ref, v_ref = maybe_kv
        kvx = _mm(c, wkvh_ref[...], wkvl_ref[...], hp)
        for pair in range(N_HEADS // 2):
            put_split(v_ref, pair, kvx[:, nq + pair * HEAD_PAD:nq + (pair + 1) * HEAD_PAD])
    for h in range(N_HEADS):
        sl = slice(h * HEAD_PAD, (h + 1) * HEAD_PAD)
        qh = qq[:, sl] * cos + qq[:, nq + h * HEAD_PAD:nq + (h + 1) * HEAD_PAD] * sin
        qh = qh * lax.rsqrt(jnp.sum(qh * qh, axis=-1, keepdims=True) * (1.0 / QK_DIM) + EPS)
        qh = qh * gqn_ref[:, sl] * qpost_ref[:, sl]
        if want_kv:
            put_split(q_ref, h, qh)
            kh = kvx[:, sl] + kpe
            kh = kh * lax.rsqrt(jnp.sum(kh * kh, axis=-1, keepdims=True) * (1.0 / QK_DIM) + EPS)
            put_split(k_ref, h, kh * gkn_ref[:, sl])
        else:
            q_ref[:, sl] = qh


def _mla_prep(zr, cos, sin, lw, qpost, *, want_kv, hp):
    t = zr.shape[0]
    tm = _tile(t, 512)
    tab_nblk = cos.shape[0] // tm
    nq = N_HEADS * HEAD_PAD
    in_specs = [pl.BlockSpec((tm, ZR_QA_BLK[0]), lambda i: (i, ZR_QA_BLK[1])),
                pl.BlockSpec((tm, ZR_KV_BLK[0]), lambda i: (i, ZR_KV_BLK[1])),
                pl.BlockSpec((tm, HEAD_PAD), lambda i: (i % tab_nblk, 0)),
                pl.BlockSpec((tm, HEAD_PAD), lambda i: (i % tab_nblk, 0)),
                _const_spec((1, Q_LORA)), _const_spec((1, KV_LORA)),
                _const_spec(lw["wq"][0].shape), _const_spec(lw["wq"][1].shape),
                _const_spec((1, nq)), _const_spec((1, nq)),
                _const_spec(lw["wkv"][0].shape), _const_spec(lw["wkv"][1].shape),
                _const_spec((1, nq))]
    args = [zr, zr, cos, sin, lw["g_qa"], lw["g_kva"], *lw["wq"], lw["g_qn"], qpost, *lw["wkv"], lw["g_kn"]]
    q_w, q_dt = (2 * nq, BF16) if want_kv else (nq, F32)
    out_specs = [pl.BlockSpec((tm, q_w), lambda i: (i, 0)),
                 pl.BlockSpec((tm, KV_LORA), lambda i: (i, 0)),
                 pl.BlockSpec((tm, HEAD_PAD), lambda i: (i, 0))]
    out_shape = [jax.ShapeDtypeStruct((t, q_w), q_dt),
                 jax.ShapeDtypeStruct((t, KV_LORA), F32),
                 jax.ShapeDtypeStruct((t, HEAD_PAD), F32)]
    if want_kv:
        out_specs += [pl.BlockSpec((tm, 2 * nq), lambda i: (i, 0)),
                      pl.BlockSpec((tm, nq), lambda i: (i, 0))]
        out_shape += [jax.ShapeDtypeStruct((t, 2 * nq), BF16),
                      jax.ShapeDtypeStruct((t, nq), BF16)]
    return pl.pallas_call(
        functools.partial(_mla_prep_kernel, want_kv=want_kv, hp=hp),
        grid=(t // tm,),
        in_specs=in_specs, out_specs=out_specs, out_shape=out_shape,
        compiler_params=_cparams("parallel"),
        name="mla_prep",
    )(*args)


def _mla_prompt_kernel(qi_ref, ki_ref, q_ref, k_ref, v_ref, o_ref, m_s, l_s, acc_s, *, tq, sub):
    t = pl.program_id(2)
    qi = qi_ref[t]
    ki = ki_ref[t]

    @pl.when(ki == 0)
    def _():
        m_s[...] = jnp.full(m_s.shape, -jnp.inf, F32)
        l_s[...] = jnp.zeros(l_s.shape, F32)
        acc_s[...] = jnp.zeros(acc_s.shape, F32)

    def step(diagonal):
        for j in range(2):
            base = 2 * j * HEAD_PAD
            for r in range(tq // sub):
                rows = slice(r * sub, (r + 1) * sub)
                n_k = (r + 1) * sub if diagonal else tq
                k_hi = k_ref[0:n_k, base:base + HEAD_PAD]
                k_lo = k_ref[0:n_k, base + HEAD_PAD:base + 2 * HEAD_PAD]
                vv = v_ref[0:n_k, :]
                q2 = q_ref[rows, base:base + 2 * HEAD_PAD]
                s = (lax.dot_general(q2, jnp.concatenate([k_hi, k_hi], axis=1), _NT,
                                     preferred_element_type=F32)
                     + lax.dot_general(q2[:, :HEAD_PAD], k_lo, _NT, preferred_element_type=F32))
                if diagonal:
                    row = r * sub + lax.broadcasted_iota(jnp.int32, (sub, n_k), 0)
                    col = lax.broadcasted_iota(jnp.int32, (sub, n_k), 1)
                    s = jnp.where(col <= row, s, -jnp.inf)
                m_prev = m_s[j, rows]
                m_new = jnp.maximum(m_prev, jnp.max(s, axis=-1, keepdims=True))
                alpha = jnp.exp(m_prev - m_new)
                p = jnp.exp(s - m_new)
                l_s[j, rows] = alpha * l_s[j, rows] + jnp.sum(p, axis=-1, keepdims=True)
                pv = jnp.dot(p.astype(BF16), vv, preferred_element_type=F32)
                acc_s[j, rows] = alpha * acc_s[j, rows] + (pv[:, :HEAD_PAD] + pv[:, HEAD_PAD:])
                m_s[j, rows] = m_new

    @pl.when(ki < qi)
    def _():
        step(False)

    @pl.when(ki == qi)
    def _():
        step(True)
        lane = lax.broadcasted_iota(jnp.int32, (tq, 2 * V_HEAD), 1)
        o_ref[...] = jnp.where(lane < V_HEAD, acc_s[0] / l_s[0], acc_s[1] / l_s[1])


def _mla_prompt(q2, k2, v2, *, nseq, seqlen):
    tq = _tile(seqlen, 1024)
    sub = _tile(tq, 256)
    nq = seqlen // tq
    qi_np = np.array([i for i in range(nq) for _ in range(i + 1)], np.int32)
    ki_np = np.array([j for i in range(nq) for j in range(i + 1)], np.int32)
    n_tri = int(qi_np.shape[0])
    pair_w = 4 * HEAD_PAD
    grid_spec = pltpu.PrefetchScalarGridSpec(
        num_scalar_prefetch=2,
        grid=(nseq, N_HEADS // 2, n_tri),
        in_specs=[pl.BlockSpec((tq, pair_w), lambda b, hp_, t, qi, ki: (b * nq + qi[t], hp_)),
                  pl.BlockSpec((tq, pair_w), lambda b, hp_, t, qi, ki: (b * nq + ki[t], hp_)),
                  pl.BlockSpec((tq, 2 * HEAD_PAD), lambda b, hp_, t, qi, ki: (b * nq + ki[t], hp_))],
        out_specs=pl.BlockSpec((tq, 2 * V_HEAD), lambda b, hp_, t, qi, ki: (b * nq + qi[t], hp_)),
        scratch_shapes=[pltpu.VMEM((2, tq, 1), F32), pltpu.VMEM((2, tq, 1), F32),
                        pltpu.VMEM((2, tq, 2 * V_HEAD), F32)])
    return pl.pallas_call(
        functools.partial(_mla_prompt_kernel, tq=tq, sub=sub),
        grid_spec=grid_spec,
        out_shape=jax.ShapeDtypeStruct((nseq * seqlen, N_HEADS * V_HEAD), F32),
        compiler_params=_cparams("parallel", "parallel", "arbitrary"),
        name="mla_prompt",
    )(jnp.asarray(qi_np), jnp.asarray(ki_np), q2, k2, v2)


def _mla_sample_kernel(pt_ref, q_ref, ckv_hbm, kpe_hbm, cn_ref, kn_ref, wkt_ref, wktp_ref, fold_ref, wv_ref,
                       o_ref, lhs_s, qr_s, m_s, l_s, ctx_s, cbuf, kbuf, sem, *, layer, pps, cpp, n_new):
    b = pl.program_id(0)
    s_idx = pl.program_id(1)
    n_steps = pl.num_programs(1)
    n_rows = N_HEADS * n_new
    g = b * n_steps + s_idx
    slot = g % 2

    def page_copies(bb, ss, sl):
        cps = []
        for i in range(pps):
            page = pt_ref[bb, ss * pps + i]
            cps.append(pltpu.make_async_copy(ckv_hbm.at[layer, page],
                                             cbuf.at[sl, pl.ds(i * PAGE, PAGE), :], sem.at[0, sl]))
            cps.append(pltpu.make_async_copy(kpe_hbm.at[layer, page],
                                             kbuf.at[sl, :, pl.ds(i * PAGE, PAGE)], sem.at[1, sl]))
        return cps

    @pl.when(g == 0)
    def _():
        for cp in page_copies(b, s_idx, slot):
            cp.start()

    @pl.when(g + 1 < pl.num_programs(0) * n_steps)
    def _():
        for cp in page_copies((g + 1) // n_steps, (g + 1) % n_steps, 1 - slot):
            cp.start()

    for cp in page_copies(b, s_idx, slot):
        cp.wait()

    @pl.when(s_idx == 0)
    def _():
        q = q_ref[...]
        lane_head = lax.broadcasted_iota(jnp.int32, q.shape, 1) // HEAD_PAD
        qblk = jnp.concatenate([jnp.where(lane_head == h, q, 0.0) for h in range(N_HEADS)], axis=0)
        qblk = qblk.astype(BF16)
        lhs_s[0:N_HEADS * QK_NOPE, :] = wkt_ref[...]
        lhs_s[N_HEADS * QK_NOPE:, :] = jnp.dot(qblk, wktp_ref[...], preferred_element_type=F32).astype(BF16)
        qr_s[...] = jnp.dot(qblk, fold_ref[...], preferred_element_type=F32).astype(BF16)
        m_s[...] = jnp.full(m_s.shape, -jnp.inf, F32)
        l_s[...] = jnp.zeros(l_s.shape, F32)
        ctx_s[...] = jnp.zeros(ctx_s.shape, F32)

    def scores(c, sr, ssr):
        cb = c.astype(BF16)
        big = lax.dot_general(lhs_s[...], cb, _NT, preferred_element_type=F32)
        pieces = []
        for h in range(N_HEADS):
            kn = big[h * QK_NOPE:(h + 1) * QK_NOPE]
            ss = jnp.sum(kn * kn, axis=0, keepdims=True) + ssr
            rk = lax.rsqrt(ss * (1.0 / QK_DIM) + EPS)
            rows = slice(N_HEADS * QK_NOPE + h * n_new, N_HEADS * QK_NOPE + (h + 1) * n_new)
            pieces.append((big[rows] + sr[h * n_new:(h + 1) * n_new]) * rk)
        return cb, jnp.concatenate(pieces, axis=0)

    def update(chunks):
        m_new = m_s[...]
        for _, sc in chunks:
            m_new = jnp.maximum(m_new, jnp.max(sc, axis=-1, keepdims=True))
        alpha = jnp.exp(m_s[...] - m_new)
        l_new = alpha * l_s[...]
        ctx = alpha * ctx_s[...]
        for cb, sc in chunks:
            p = jnp.exp(sc - m_new)
            l_new = l_new + jnp.sum(p, axis=-1, keepdims=True)
            ctx = ctx + jnp.dot(p.astype(BF16), cb, preferred_element_type=F32)
        l_s[...] = l_new
        ctx_s[...] = ctx
        m_s[...] = m_new

    chunks = []
    for j in range(0, pps, cpp):
        keys = pl.ds(j * PAGE, cpp * PAGE)
        kpe_t = kbuf[slot, :, keys]
        chunks.append(scores(cbuf[slot, keys, :],
                             jnp.dot(qr_s[...], kpe_t.astype(BF16), preferred_element_type=F32),
                             jnp.sum(kpe_t * kpe_t, axis=0, keepdims=True)))
    update(chunks)

    @pl.when(s_idx == pl.num_programs(1) - 1)
    def _():
        cpad = jnp.concatenate([cn_ref[...], jnp.zeros((PAGE - n_new, KV_LORA), F32)], axis=0)
        kpad = jnp.concatenate([kn_ref[...], jnp.zeros((PAGE - n_new, QK_ROPE), F32)], axis=0)
        key = lax.broadcasted_iota(jnp.int32, (n_rows, PAGE), 1)
        qry = lax.broadcasted_iota(jnp.int32, (n_rows, PAGE), 0) % n_new
        sr = lax.dot_general(qr_s[...], kpad.astype(BF16), _NT, preferred_element_type=F32)
        ones = jnp.ones((8, QK_ROPE), BF16)
        ssr = lax.dot_general(ones, (kpad * kpad).astype(BF16), _NT, preferred_element_type=F32)[0:1]
        cb, sc = scores(cpad, sr, ssr)
        update([(cb, jnp.where(key <= qry, sc, -jnp.inf))])
        ctx = (ctx_s[...] / l_s[...]).astype(BF16)
        res = jnp.dot(ctx, wv_ref[...], preferred_element_type=F32)
        lane_head = lax.broadcasted_iota(jnp.int32, (n_new, N_HEADS * V_HEAD), 1) // V_HEAD
        out = jnp.zeros((n_new, N_HEADS * V_HEAD), F32)
        for h in range(N_HEADS):
            out = out + jnp.where(lane_head == h, res[h * n_new:(h + 1) * n_new], 0.0)
        o_ref[...] = out


def _mla_sample(q, c_new, kpe_new, cache_ckv, cache_kpe_t, page_table, layer, lw):
    nseq, n_new, _ = q.shape
    n_pages = page_table.shape[1]
    pps = _tile(n_pages, 64)
    n_steps = n_pages // pps
    n_rows = N_HEADS * n_new

    in_specs = [pl.BlockSpec((None, n_new, N_HEADS * HEAD_PAD), lambda b, s, pt: (b, 0, 0)),
                pl.BlockSpec(memory_space=pl.ANY), pl.BlockSpec(memory_space=pl.ANY),
                pl.BlockSpec((None, n_new, KV_LORA), lambda b, s, pt: (b, 0, 0)),
                pl.BlockSpec((None, n_new, QK_ROPE), lambda b, s, pt: (b, 0, 0)),
                _const_spec(lw["wkt"].shape), _const_spec(lw["wktp"].shape),
                _const_spec(lw["fold"].shape), _const_spec(lw["wv"].shape)]
    grid_spec = pltpu.PrefetchScalarGridSpec(
        num_scalar_prefetch=1,
        grid=(nseq, n_steps),
        in_specs=in_specs,
        out_specs=pl.BlockSpec((None, n_new, N_HEADS * V_HEAD), lambda b, s, pt: (b, 0, 0)),
        scratch_shapes=[pltpu.VMEM((N_HEADS * QK_NOPE + n_rows, KV_LORA), BF16),
                        pltpu.VMEM((n_rows, QK_ROPE), BF16),
                        pltpu.VMEM((n_rows, 1), F32), pltpu.VMEM((n_rows, 1), F32),
                        pltpu.VMEM((n_rows, KV_LORA), F32),
                        pltpu.VMEM((2, pps * PAGE, KV_LORA), F32),
                        pltpu.VMEM((2, QK_ROPE, pps * PAGE), F32),
                        pltpu.SemaphoreType.DMA((2, 2))])
    return pl.pallas_call(
        functools.partial(_mla_sample_kernel, layer=layer, pps=pps, cpp=_tile(pps, 4), n_new=n_new),
        grid_spec=grid_spec,
        out_shape=jax.ShapeDtypeStruct((nseq, n_new, N_HEADS * V_HEAD), F32),
        compiler_params=_cparams("arbitrary", "arbitrary"),
        name="mla_sample",
    )(page_table, q, cache_ckv, cache_kpe_t, c_new, kpe_new,
      lw["wkt"], lw["wktp"], lw["fold"], lw["wv"])


def _memkv_kernel(x_ref, g_ref, wh_ref, wl_ref, gk_ref, k_ref, v_ref, *, hp):
    kv = _mm(_rms(x_ref[...], g_ref[...]), wh_ref[...], wl_ref[...], hp)
    for h in range(MEM_HEADS):
        sl = slice(h * MEM_HD, (h + 1) * MEM_HD)
        k_ref[:, sl] = _rms(kv[:, sl], gk_ref[...])
    v_ref[...] = kv[:, MEM_W:]


def _memkv(mem, g_mem, wmkv_hl, g_mkn, *, hp):
    t = mem.shape[0]
    tm = _tile(t, 256)
    return pl.pallas_call(
        functools.partial(_memkv_kernel, hp=hp),
        grid=(t // tm,),
        in_specs=[pl.BlockSpec((tm, D_MODEL), lambda i: (i, 0)),
                  _const_spec((1, D_MODEL)), _const_spec(wmkv_hl[0].shape), _const_spec(wmkv_hl[1].shape),
                  _const_spec((1, MEM_HD))],
        out_specs=[pl.BlockSpec((tm, MEM_W), lambda i: (i, 0))] * 2,
        out_shape=[jax.ShapeDtypeStruct((t, MEM_W), F32)] * 2,
        compiler_params=_cparams("parallel"),
        name="memkv",
    )(mem, g_mem, *wmkv_hl, g_mkn)


def _memattn_kernel(q_ref, k_ref, v_ref, g_ref, o_ref, *, hp, nb, ts, rows_by_head):
    scale = 1.0 / math.sqrt(MEM_HD)
    for i in range(nb):
        rows = slice(i * ts, (i + 1) * ts)
        for h in range(MEM_HEADS):
            sl = slice(h * MEM_HD, (h + 1) * MEM_HD)
            if rows_by_head:
                k = k_ref[i, pl.ds(h, N_MEM, stride=MEM_HEADS), :]
                v = v_ref[i, pl.ds(h, N_MEM, stride=MEM_HEADS), :]
            else:
                k = k_ref[:, sl]
                v = v_ref[:, sl]
            q = _rms(q_ref[rows, sl], g_ref[...]) * scale
            s = _mm2(q, k, hp, _NT)
            p = jnp.exp(s - jnp.max(s, axis=-1, keepdims=True))
            l = jnp.sum(p, axis=-1, keepdims=True)
            o_ref[rows, sl] = _mm2(p, v, hp) / l


def _memattn(zr, mk, mv, g_mqn, *, nseq, seqlen, kv_index, hp):
    rows_by_head = mk.shape[-1] == MEM_HD
    if rows_by_head:
        nb, ts, nq = _tile(nseq, 8), seqlen, 1
        kv_blk = (None,) * (mk.ndim - 3) + (nb, N_MEM * MEM_HEADS, MEM_HD)
    else:
        nb, ts = 1, _tile(seqlen, 512)
        nq = seqlen // ts
        kv_blk = (N_MEM, MEM_W)
    return pl.pallas_call(
        functools.partial(_memattn_kernel, hp=hp, nb=nb, ts=ts, rows_by_head=rows_by_head),
        grid=(nseq // nb, nq),
        in_specs=[pl.BlockSpec((nb * ts, ZR_MQ_BLK[0]), lambda b, i: (b * nq + i, ZR_MQ_BLK[1])),
                  pl.BlockSpec(kv_blk, lambda b, i: kv_index(b)),
                  pl.BlockSpec(kv_blk, lambda b, i: kv_index(b)),
                  _const_spec((1, MEM_HD))],
        out_specs=pl.BlockSpec((nb * ts, MEM_W), lambda b, i: (b * nq + i, 0)),
        out_shape=jax.ShapeDtypeStruct((nseq * seqlen, MEM_W), F32),
        compiler_params=_cparams("parallel", "parallel"),
        name="memattn",
    )(zr, mk, mv, g_mqn)


def _merge_kernel(x_ref, ya_ref, yb_ref, yc_ref, ym_ref, zg_ref, wbrh_ref, wbrl_ref, woh_ref, wol_ref,
                  gf_ref, wrh_ref, wrl_ref, br_ref, x1_ref, h2_ref, cw_ref, *, hp):
    acc = None
    for b, y_ref in enumerate((ya_ref, yb_ref, yc_ref, ym_ref)):
        term = zg_ref[:, b * D_MODEL:(b + 1) * D_MODEL] * _mm(y_ref[...], wbrh_ref[b], wbrl_ref[b], hp)
        acc = term if acc is None else acc + term
    x1 = x_ref[...] + _mm(acc, woh_ref[...], wol_ref[...], hp)
    x1_ref[...] = x1
    h2 = _rms(x1, gf_ref[...])
    h2_ref[...] = h2
    logits = _mm(h2, wrh_ref[...], wrl_ref[...], True) + br_ref[...]
    lane = lax.broadcasted_iota(jnp.int32, logits.shape, 1).astype(F32)
    big = 1e9
    is_grp = (lane >= N_EXPERTS) & (lane < N_EXPERTS + N_GROUPS)
    gl = jnp.where(is_grp, logits, -jnp.inf)
    ge = jnp.exp(gl - jnp.max(gl, axis=-1, keepdims=True))
    gprob = ge / jnp.sum(ge, axis=-1, keepdims=True)
    gp = jnp.max(gprob, axis=-1, keepdims=True)
    gi = jnp.min(jnp.where(is_grp & (gprob == gp), lane, big), axis=-1, keepdims=True) - N_EXPERTS
    sel = (lane >= gi * EXP_PER_GROUP) & (lane < (gi + 1) * EXP_PER_GROUP)
    el = jnp.where(sel, logits, -jnp.inf)
    ee = jnp.exp(el - jnp.max(el, axis=-1, keepdims=True))
    eprob = ee / jnp.sum(ee, axis=-1, keepdims=True)
    v1 = jnp.max(eprob, axis=-1, keepdims=True)
    i1 = jnp.min(jnp.where(sel & (eprob == v1), lane, big), axis=-1, keepdims=True)
    rest = jnp.where(sel & (lane != i1), eprob, -1.0)
    v2 = jnp.max(rest, axis=-1, keepdims=True)
    i2 = jnp.min(jnp.where(rest == v2, lane, big), axis=-1, keepdims=True)
    den = v1 + v2
    cw = jnp.where(lane == i1, gp * v1 / den, jnp.where(lane == i2, gp * v2 / den, 0.0))
    cw_ref[...] = jnp.where(lane == GROUP_LANE, gi, cw)


def _merge(x, ya, yb, yc, ym, zg, lw, *, hp):
    t = x.shape[0]
    tm = _tile(t, 512)
    row = lambda w: pl.BlockSpec((tm, w), lambda i: (i, 0))
    consts = [*lw["wbr"], *lw["wo"], lw["g_ffn"], *lw["wr"], lw["b_r"]]
    return pl.pallas_call(
        functools.partial(_merge_kernel, hp=hp),
        grid=(t // tm,),
        in_specs=[row(D_MODEL), row(POOL_W), row(512), row(SG_W), row(MEM_W), row(N_BRANCH * D_MODEL)]
                 + [_const_spec(c.shape) for c in consts],
        out_specs=[row(D_MODEL), row(D_MODEL), row(128)],
        out_shape=[jax.ShapeDtypeStruct((t, D_MODEL), F32),
                   jax.ShapeDtypeStruct((t, D_MODEL), F32),
                   jax.ShapeDtypeStruct((t, 128), F32)],
        compiler_params=_cparams("parallel"),
        name="merge",
    )(x, ya, yb, yc, ym, zg, *consts)


def _moe_kernel(h_ref, cw_ref, x_ref, wguh_ref, wgul_ref, wdh_ref, wdl_ref, o_ref, acc_s, hh_s, hl_s, *, hp):
    e = pl.program_id(1)

    @pl.when(e == 0)
    def _():
        acc_s[...] = jnp.zeros(acc_s.shape, F32)
        hh_s[...], hl_s[...] = _split(h_ref[...])

    dot = lambda a, b: jnp.dot(a, b, preferred_element_type=F32)
    ab = dot(hh_s[...], wguh_ref[...])
    if hp:
        ab = ab + (dot(hl_s[...], wguh_ref[...]) + dot(hh_s[...], wgul_ref[...]))
    cw = cw_ref[...]
    lane = lax.broadcasted_iota(jnp.int32, cw.shape, 1)
    cwe = jnp.sum(jnp.where(lane == e, cw, 0.0), axis=-1, keepdims=True)
    t = jax.nn.silu(ab[:, :D_EXPERT]) * ab[:, D_EXPERT:] * cwe
    acc_s[...] += _mm(t, wdh_ref[...], wdl_ref[...], hp)

    @pl.when(e == N_EXPERTS - 1)
    def _():
        o_ref[...] = x_ref[...] + acc_s[...]


def _moe(h2, cw, x1, wgu_hl, wd_hl, *, hp):
    t = h2.shape[0]
    tm = _tile(t, 512)
    wgu_spec = pl.BlockSpec((None, D_MODEL, 2 * D_EXPERT), lambda i, e: (e, 0, 0))
    wd_spec = pl.BlockSpec((None, D_EXPERT, D_MODEL), lambda i, e: (e, 0, 0))
    return pl.pallas_call(
        functools.partial(_moe_kernel, hp=hp),
        grid=(t // tm, N_EXPERTS),
        in_specs=[pl.BlockSpec((tm, D_MODEL), lambda i, e: (i, 0)),
                  pl.BlockSpec((tm, 128), lambda i, e: (i, 0)),
                  pl.BlockSpec((tm, D_MODEL), lambda i, e: (i, 0)),
                  wgu_spec, wgu_spec, wd_spec, wd_spec],
        out_specs=pl.BlockSpec((tm, D_MODEL), lambda i, e: (i, 0)),
        out_shape=jax.ShapeDtypeStruct((t, D_MODEL), F32),
        scratch_shapes=[pltpu.VMEM((tm, D_MODEL), F32), pltpu.VMEM((tm, D_MODEL), BF16),
                        pltpu.VMEM((tm, D_MODEL), BF16)],
        compiler_params=_cparams("parallel", "arbitrary"),
        name="moe",
    )(h2, cw, x1, *wgu_hl, *wd_hl)


def _split3(a):
    mask = lambda v: lax.bitcast_convert_type(lax.bitcast_convert_type(v, jnp.int32) & jnp.int32(-65536), F32)
    p1 = mask(a)
    r1 = a - p1
    p2 = mask(r1)
    return p1.astype(BF16), p2.astype(BF16), (r1 - p2).astype(BF16)


def _moe_sorted_kernel(h_ref, cw_ref, x_ref, wguh_ref, wgul_ref, wdh_ref, wdl_ref, o_ref,
                       xh_s, xl_s, cws_s, ys_s, pt_s, seg_s, *, tb, ch, eps):
    step = pl.program_id(1)
    e0 = step * eps
    dot = lambda a, b: jnp.dot(a, b, preferred_element_type=F32)

    @pl.when(step == 0)
    def _():
        cw = cw_ref[...]
        lane = lax.broadcasted_iota(jnp.int32, (tb, 128), 1)
        gi = jnp.sum(jnp.where(lane == GROUP_LANE, cw, 0.0), axis=-1, keepdims=True)
        oh = jnp.where(lane.astype(F32) == gi, 1.0, 0.0)
        r = lax.broadcasted_iota(jnp.int32, (tb, tb), 0)
        c = lax.broadcasted_iota(jnp.int32, (tb, tb), 1)
        lower = jnp.where(c < r, 1.0, 0.0).astype(BF16)
        rank = dot(lower, oh.astype(BF16))
        cnt = jnp.sum(oh, axis=0, keepdims=True)
        lane_row = lax.broadcasted_iota(jnp.int32, (1, 128), 1)
        off = jnp.zeros((1, 128), F32)
        for g in range(N_GROUPS - 1):
            off = off + jnp.where(lane_row > g, cnt[:, g:g + 1], 0.0)
        slot = jnp.sum(oh * (off + rank), axis=-1, keepdims=True)
        pt_s[...] = jnp.where(slot == c.astype(F32), 1.0, 0.0).astype(BF16)
        d_hi = jnp.floor(slot * (1.0 / 32.0))
        digits = jnp.where(lane == 0, d_hi, jnp.where(lane == 1, slot - 32.0 * d_hi, 0.0)).astype(BF16)
        er = lax.broadcasted_iota(jnp.int32, (128, 128), 0)
        ec = lax.broadcasted_iota(jnp.int32, (128, 128), 1)
        eye = jnp.where(er == ec, 1.0, 0.0).astype(BF16)
        dt = lax.dot_general(eye, digits, _NT, preferred_element_type=F32)
        slot_row = 32.0 * dt[0:1] + dt[1:2]
        p = jnp.where(slot_row == r.astype(F32), 1.0, 0.0).astype(BF16)
        h_hi, h_lo = _split(h_ref[...])
        xh_s[...] = dot(p, h_hi).astype(BF16)
        xl_s[...] = dot(p, h_lo).astype(BF16)
        c1, c2, c3 = _split3(cw)
        cws_s[...] = dot(p, c1) + (dot(p, c2) + dot(p, c3))
        ys_s[...] = jnp.zeros(ys_s.shape, F32)
        for g in range(N_GROUPS):
            seg_s[g] = off[0, g].astype(jnp.int32)
            seg_s[N_GROUPS + g] = (off[0, g] + cnt[0, g]).astype(jnp.int32)

    grp = e0 // EXP_PER_GROUP
    start = seg_s[grp]
    end = seg_s[N_GROUPS + grp]
    w0 = (start // 16) * 16
    n_win = jnp.where(end > start, (end - w0 + ch - 1) // ch, 0)

    def window(j, carry):
        lo = w0 + j * ch
        ws = pl.multiple_of(jnp.minimum(lo, tb - ch), 16)
        rows = pl.ds(ws, ch)
        xh = xh_s[rows, :]
        xl = xl_s[rows, :]
        cws = cws_s[rows, :]
        lane = lax.broadcasted_iota(jnp.int32, cws.shape, 1)
        row = ws + lax.broadcasted_iota(jnp.int32, (ch, 1), 0)
        y = None
        for j in range(eps):
            ab = dot(xh, wguh_ref[j]) + (dot(xl, wguh_ref[j]) + dot(xh, wgul_ref[j]))
            cwe = jnp.sum(jnp.where(lane == e0 + j, cws, 0.0), axis=-1, keepdims=True)
            cwe = jnp.where(row >= lo, cwe, 0.0)
            t = jax.nn.silu(ab[:, :D_EXPERT]) * ab[:, D_EXPERT:] * cwe
            yj = _mm(t, wdh_ref[j], wdl_ref[j], True)
            y = yj if y is None else y + yj
        ys_s[rows, :] += y
        return carry

    lax.fori_loop(0, n_win, window, 0)

    @pl.when(step == pl.num_programs(1) - 1)
    def _():
        y_hi, y_lo = _split(ys_s[...])
        pt = pt_s[...]
        o_ref[...] = x_ref[...] + (dot(pt, y_hi) + dot(pt, y_lo))


def _moe_sorted(h2, cw, x1, wgu_hl, wd_hl):
    t = h2.shape[0]
    tb = 1024
    once = lambda w: pl.BlockSpec((tb, w), lambda i, e: (i, 0), pipeline_mode=pl.Buffered(1))
    eps = 2
    wgu_spec = pl.BlockSpec((eps, D_MODEL, 2 * D_EXPERT), lambda i, e: (e, 0, 0))
    wd_spec = pl.BlockSpec((eps, D_EXPERT, D_MODEL), lambda i, e: (e, 0, 0))
    return pl.pallas_call(
        functools.partial(_moe_sorted_kernel, tb=tb, ch=320, eps=eps),
        grid=(t // tb, N_EXPERTS // eps),
        in_specs=[once(D_MODEL), once(128), once(D_MODEL), wgu_spec, wgu_spec, wd_spec, wd_spec],
        out_specs=pl.BlockSpec((tb, D_MODEL), lambda i, e: (i, 0)),
        out_shape=jax.ShapeDtypeStruct((t, D_MODEL), F32),
        scratch_shapes=[pltpu.VMEM((tb, D_MODEL), BF16), pltpu.VMEM((tb, D_MODEL), BF16),
                        pltpu.VMEM((tb, 128), F32), pltpu.VMEM((tb, D_MODEL), F32),
                        pltpu.VMEM((tb, tb), BF16), pltpu.SMEM((2 * N_GROUPS,), jnp.int32)],
        compiler_params=_cparams("parallel", "arbitrary"),
        name="moe_sorted",
    )(h2, cw, x1, *wgu_hl, *wd_hl)


def _hl(w):
    b = lax.bitcast_convert_type(w, jnp.int32)
    b = (b + jnp.int32(0x7FFF) + ((b >> 16) & 1)) & jnp.int32(-65536)
    hi = lax.bitcast_convert_type(b, F32)
    return hi.astype(BF16), (w - hi).astype(BF16)


def _rope_tables(pos):
    half = QK_ROPE // 2
    inv = ROPE_THETA ** (-jnp.arange(half, dtype=F32) / half)
    ang = pos.astype(F32)[:, None] * inv
    cos, sin = jnp.cos(ang), jnp.sin(ang)
    n = pos.shape[0]
    cos128 = jnp.concatenate([jnp.ones((n, QK_NOPE), F32), cos, cos, jnp.zeros((n, 32), F32)], 1)
    sin128 = jnp.concatenate([jnp.zeros((n, QK_NOPE), F32), sin, sin, jnp.zeros((n, 32), F32)], 1)
    return cos128, sin128


def _fold_matrix():
    f = np.zeros((N_HEADS * HEAD_PAD, QK_ROPE), np.float32)
    for h in range(N_HEADS):
        for j in range(QK_ROPE):
            f[h * HEAD_PAD + QK_NOPE + j, j] = 1.0
    return jnp.asarray(f, BF16)


def _head_pad_gain(g):
    return jnp.tile(jnp.concatenate([g, jnp.zeros((HEAD_PAD - QK_DIM,), F32)]), N_HEADS)[None]


def _layer_weights(l, w):
    half = QK_ROPE // 2
    wi = w["w_in"][l]
    w_a, w_qa = wi[:, 0:512], wi[:, 512:768]
    w_lat, w_rope = wi[:, 768:896], wi[:, 896:928]
    w_c, w_mq, w_g = wi[:, 928:1952], wi[:, 1952:2464], wi[:, 2464:]
    z64 = jnp.zeros((D_MODEL, QK_NOPE), F32)
    z32 = jnp.zeros((D_MODEL, 32), F32)
    rope128 = jnp.concatenate([z64, w_rope, z32], 1)
    ropesw128 = jnp.concatenate([z64, -w_rope[:, half:], w_rope[:, :half], z32], 1)
    lw = {}
    lw["wr_in"] = _hl(jnp.concatenate([w_c, w_a, w_mq, w_qa, w_lat, rope128, ropesw128], 1))
    lw["wg_in"] = _hl(w_g)
    lw["g_mix"] = w["g_mix"][l][None]

    wq = w["w_qb"][l].reshape(Q_LORA, N_HEADS, QK_DIM)
    nope, x1, x2 = wq[..., :QK_NOPE], wq[..., QK_NOPE:QK_NOPE + half], wq[..., QK_NOPE + half:]
    zq = lambda n: jnp.zeros((Q_LORA, N_HEADS, n), F32)
    wq1 = jnp.concatenate([nope, x1, x2, zq(32)], -1).reshape(Q_LORA, N_HEADS * HEAD_PAD)
    wq2 = jnp.concatenate([zq(QK_NOPE), -x2, x1, zq(32)], -1).reshape(Q_LORA, N_HEADS * HEAD_PAD)
    lw["wq"] = _hl(jnp.concatenate([wq1, wq2], 1))
    lw["g_qa"] = w["g_qa"][l][None]
    lw["g_kva"] = w["g_kva"][l][None]
    lw["g_qn"] = _head_pad_gain(w["g_qn"][l])
    lw["g_kn"] = _head_pad_gain(w["g_kn"][l])

    wkv = w["w_kvb"][l].reshape(KV_LORA, N_HEADS, QK_NOPE + V_HEAD)
    kn, vv = wkv[..., :QK_NOPE], wkv[..., QK_NOPE:]
    wk_pad = jnp.concatenate([kn, jnp.zeros((KV_LORA, N_HEADS, HEAD_PAD - QK_NOPE), F32)], -1)
    wk_pad = wk_pad.reshape(KV_LORA, N_HEADS * HEAD_PAD)
    wv = vv.reshape(KV_LORA, N_HEADS * V_HEAD)
    lw["wkv"] = _hl(jnp.concatenate([wk_pad, wv], 1))
    lw["wkt"] = kn.reshape(KV_LORA, N_HEADS * QK_NOPE).T.astype(BF16)
    lw["wktp"] = wk_pad.T.astype(BF16)
    lw["wv"] = wv.astype(BF16)
    lw["fold"] = _fold_matrix()

    lw["w_pool"] = _hl(w["w_pool"][l])
    lw["pool_scale"] = w["pool_scale"][l][None]
    lw["g_sgv"] = w["g_sgv"][l]
    lw["w_sp"] = w["w_sp"][l]
    lw["b_sp"] = w["b_sp"][l]
    lw["g_mem"] = w["g_mem"][l][None]
    lw["wmkv"] = _hl(jnp.concatenate([w["w_mk"][l], w["w_mv"][l]], 1))
    lw["g_mqn"] = w["g_mqn"][l][None]
    lw["g_mkn"] = w["g_mkn"][l][None]
    lw["wbr"] = _hl(w["w_br"][l])
    lw["wo"] = _hl(w["w_o"][l])
    lw["g_ffn"] = w["g_ffn"][l][None]
    lw["wr"] = _hl(jnp.concatenate([w["w_re"][l], w["w_rg"][l],
                                    jnp.zeros((D_MODEL, 128 - N_EXPERTS - N_GROUPS), F32)], 1))
    lw["b_r"] = jnp.concatenate([w["b_re"][l], w["b_rg"][l],
                                 jnp.zeros((128 - N_EXPERTS - N_GROUPS,), F32)])[None]
    lw["wgu"] = _hl(jnp.concatenate([w["w_eg"][l], w["w_eu"][l]], -1))
    lw["wd"] = _hl(w["w_ed"][l])
    return lw


def _sgu_operands(lw, seqlen):
    blk = min(seqlen, CHUNK)
    rep = CHUNK // blk
    ws = jnp.tile(lw["w_sp"][:, :blk, :blk], (1, rep, rep))
    b = jnp.tile(lw["b_sp"][:, :blk], (1, rep))
    bfull = jnp.repeat(b.T, CHUNK, axis=1)
    return ws, bfull, blk


def _group_layer(x, lw, *, nseq, seqlen, pos0, cos, sin, pool_buf, mem_k, mem_v, kv_index,
                 mla_sample_fn, hp):
    zr = _inproj(x, lw["g_mix"], lw["wr_in"], hp=hp, sigmoid=False)
    zg = _inproj(x, lw["g_mix"], lw["wg_in"], hp=hp, sigmoid=True)
    ya, pool_new = _pool(zr, pool_buf, lw["w_pool"], lw["pool_scale"],
                         nseq=nseq, seqlen=seqlen, pos0=pos0, hp=hp)
    ws, bfull, blk = _sgu_operands(lw, seqlen)
    yc, v_rows = _sgu(zr, lw["g_sgv"], ws, bfull, blk=blk, hp=hp)
    scale = 1.0 / math.sqrt(QK_DIM)
    if mla_sample_fn is not None:
        q, c, kpe128 = _mla_prep(zr, cos, sin, lw, lw["g_kn"] * scale, want_kv=False, hp=hp)
        kpe = kpe128[:, QK_NOPE:QK_DIM]
        yb = mla_sample_fn(q, c, kpe)
    else:
        qpost = jnp.full((1, N_HEADS * HEAD_PAD), scale, F32)
        q, c, kpe128, k, v = _mla_prep(zr, cos, sin, lw, qpost, want_kv=True, hp=hp)
        kpe = kpe128[:, QK_NOPE:QK_DIM]
        yb = _mla_prompt(q, k, v, nseq=nseq, seqlen=seqlen)
    ym = _memattn(zr, mem_k, mem_v, lw["g_mqn"], nseq=nseq, seqlen=seqlen, kv_index=kv_index, hp=hp)
    x1, h2, cw = _merge(x, ya, yb, yc, ym, zg, lw, hp=hp)
    if hp and x.shape[0] % 1024 == 0:
        x2 = _moe_sorted(h2, cw, x1, lw["wgu"], lw["wd"])
    else:
        x2 = _moe(h2, cw, x1, lw["wgu"], lw["wd"], hp=hp)
    return x2, c, kpe, pool_new, v_rows


def kernel(x_prompt, x_sample, mem_prompt, cache_ckv, cache_kpe, cache_memk, cache_memv, state_pool, page_table, g_mix, w_in, g_qa, w_qb, g_kva, w_kvb, g_qn, g_kn, w_pool, pool_scale, g_sgv, w_sp, b_sp, g_mem, w_mk, w_mv, g_mqn, g_mkn, w_br, w_o, g_ffn, w_rg, b_rg, w_re, b_re, w_eg, w_eu, w_ed):
    w = dict(g_mix=g_mix, w_in=w_in, g_qa=g_qa, w_qb=w_qb, g_kva=g_kva, w_kvb=w_kvb, g_qn=g_qn, g_kn=g_kn,
             w_pool=w_pool, pool_scale=pool_scale, g_sgv=g_sgv, w_sp=w_sp, b_sp=b_sp, g_mem=g_mem,
             w_mk=w_mk, w_mv=w_mv, g_mqn=g_mqn, g_mkn=g_mkn, w_br=w_br, w_o=w_o, g_ffn=g_ffn,
             w_rg=w_rg, b_rg=b_rg, w_re=w_re, b_re=b_re, w_eg=w_eg, w_eu=w_eu, w_ed=w_ed)
    nb, seq, _ = x_prompt.shape
    ndb, t_new, _ = x_sample.shape
    depth = w_in.shape[0]
    past = page_table.shape[1] * PAGE

    cos_p, sin_p = _rope_tables(jnp.arange(seq, dtype=jnp.int32))
    cos_s, sin_s = _rope_tables(past + jnp.arange(t_new, dtype=jnp.int32))
    tm_s = _tile(ndb * t_new, 512)
    cos_s = jnp.tile(cos_s, (tm_s // t_new, 1))
    sin_s = jnp.tile(sin_s, (tm_s // t_new, 1))

    xp = x_prompt.reshape(nb * seq, D_MODEL)
    xs = x_sample.reshape(ndb * t_new, D_MODEL)
    mem = mem_prompt.reshape(nb * N_MEM, D_MODEL)
    zero_buf = jnp.zeros((nb, POOL_PAD, POOL_W), F32)
    memk_s = cache_memk.reshape(depth, ndb, N_MEM * MEM_HEADS, MEM_HD)
    memv_s = cache_memv.reshape(depth, ndb, N_MEM * MEM_HEADS, MEM_HD)
    cache_kpe_t = jnp.swapaxes(cache_kpe, 2, 3)

    outs = [[] for _ in range(9)]
    for l in range(depth):
        lw = _layer_weights(l, w)
        mk, mv = _memkv(mem, lw["g_mem"], lw["wmkv"], lw["g_mkn"], hp=True)
        xp, c, kp, pb, _ = _group_layer(
            xp, lw, nseq=nb, seqlen=seq, pos0=0, cos=cos_p, sin=sin_p, pool_buf=zero_buf,
            mem_k=mk, mem_v=mv, kv_index=lambda b: (b, 0), mla_sample_fn=None, hp=True)
        outs[0].append(c.reshape(nb, seq, KV_LORA))
        outs[1].append(kp.reshape(nb, seq, QK_ROPE))
        outs[2].append(mk.reshape(nb, N_MEM, MEM_HEADS, MEM_HD))
        outs[3].append(mv.reshape(nb, N_MEM, MEM_HEADS, MEM_HD))
        outs[4].append(pb[:, 1:])

        def mla_sample_fn(q, c_new, kpe_new, l=l, lw=lw):
            o = _mla_sample(q.reshape(ndb, t_new, -1), c_new.reshape(ndb, t_new, KV_LORA),
                            kpe_new.reshape(ndb, t_new, QK_ROPE), cache_ckv, cache_kpe_t, page_table, l, lw)
            return o.reshape(ndb * t_new, N_HEADS * V_HEAD)

        buf_s = jnp.pad(state_pool[l], ((0, 0), (1, 0), (0, 0)))
        xs, c, kp, pb, vr = _group_layer(
            xs, lw, nseq=ndb, seqlen=t_new, pos0=past, cos=cos_s, sin=sin_s, pool_buf=buf_s,
            mem_k=memk_s, mem_v=memv_s, kv_index=lambda b, l=l: (l, b, 0, 0),
            mla_sample_fn=mla_sample_fn, hp=False)
        outs[5].append(c.reshape(ndb, t_new, KV_LORA))
        outs[6].append(kp.reshape(ndb, t_new, QK_ROPE))
        outs[7].append(pb[:, 1:])
        outs[8].append(vr.reshape(ndb, t_new, SG_W))

    stacked = [jnp.stack(o) for o in outs]
    return (xp.reshape(nb, seq, D_MODEL), xs.reshape(ndb, t_new, D_MODEL), *stacked)
```

```python
import functools
import math

import numpy as np
import jax
import jax.numpy as jnp
from jax import lax
from jax.experimental import pallas as pl
from jax.experimental.pallas import tpu as pltpu

F32 = jnp.float32
BF16 = jnp.bfloat16
EPS = 1e-6

D_MODEL = 1024
PAGE = 128
N_MEM = 256
POOL_WINDOWS = (2, 4, 8, 16)
POOL_GW = 128
POOL_W = 512
POOL_BUF = 15
POOL_PAD = 16
N_HEADS = 8
QK_NOPE = 64
QK_ROPE = 32
QK_DIM = QK_NOPE + QK_ROPE
V_HEAD = 64
HEAD_PAD = 128
Q_LORA = 256
KV_LORA = 128
ROPE_THETA = 10000.0
CHUNK = 128
SG_GROUPS = 4
SG_W = 512
MEM_HEADS = 4
MEM_HD = 128
MEM_W = 512
N_BRANCH = 4
N_GROUPS = 4
EXP_PER_GROUP = 4
N_EXPERTS = 16
D_EXPERT = 256
GROUP_LANE = N_EXPERTS

ZR_W = 2688
ZR_C_BLK = (1024, 0)
ZR_A_BLK = (512, 2)
ZR_MQ_BLK = (512, 3)
ZR_QA_BLK = (256, 8)
ZR_KV_BLK = (384, 6)

VMEM_LIMIT = 56 * 1024 * 1024

_NN = (((1,), (0,)), ((), ()))
_NT = (((1,), (1,)), ((), ()))


def _tile(n, pref):
    return pref if n % pref == 0 else n


def _cparams(*sem):
    return pltpu.CompilerParams(dimension_semantics=sem, vmem_limit_bytes=VMEM_LIMIT)


def _rms(x, g):
    return x * lax.rsqrt(jnp.mean(x * x, axis=-1, keepdims=True) + EPS) * g


def _const_spec(shape):
    nd = len(shape)
    return pl.BlockSpec(shape, lambda *_: (0,) * nd, pipeline_mode=pl.Buffered(1))


def _split(a):
    hi = lax.bitcast_convert_type(lax.bitcast_convert_type(a, jnp.int32) & jnp.int32(-65536), F32)
    return hi.astype(BF16), (a - hi).astype(BF16)


def _mm(a, w_hi, w_lo, hp, dims=_NN):
    dot = lambda x, y: lax.dot_general(x, y, dims, preferred_element_type=F32)
    if not hp:
        return dot(a.astype(BF16), w_hi)
    a_hi, a_lo = _split(a)
    return dot(a_hi, w_hi) + (dot(a_lo, w_hi) + dot(a_hi, w_lo))


def _mm2(a, b, hp, dims=_NN):
    if not hp:
        return lax.dot_general(a.astype(BF16), b.astype(BF16), dims, preferred_element_type=F32)
    b_hi, b_lo = _split(b)
    return _mm(a, b_hi, b_lo, True, dims)


def _inproj_kernel(x_ref, g_ref, wh_ref, wl_ref, z_ref, *, hp, sigmoid):
    h = _rms(x_ref[...], g_ref[...])
    z = _mm(h, wh_ref[...], wl_ref[...], hp)
    z_ref[...] = jax.nn.sigmoid(z) if sigmoid else z


def _inproj(x, g, w_hl, *, hp, sigmoid):
    t = x.shape[0]
    tm = _tile(t, 256)
    n = w_hl[0].shape[1]
    return pl.pallas_call(
        functools.partial(_inproj_kernel, hp=hp, sigmoid=sigmoid),
        grid=(t // tm,),
        in_specs=[pl.BlockSpec((tm, D_MODEL), lambda i: (i, 0)),
                  _const_spec((1, D_MODEL)),
                  _const_spec(w_hl[0].shape), _const_spec(w_hl[1].shape)],
        out_specs=pl.BlockSpec((tm, n), lambda i: (i, 0)),
        out_shape=jax.ShapeDtypeStruct((t, n), F32),
        compiler_params=_cparams("parallel"),
        name="inproj",
    )(x, g, *w_hl)


def _pool_kernel(a_ref, buf_ref, wph_ref, wpl_ref, ps_ref, y_ref, pn_ref, ext_ref, *, nb, ts, pos0, hp):
    s = pl.program_id(1)

    @pl.when(s == 0)
    def _():
        ext_ref[:, 0:POOL_PAD, :] = buf_ref[...]

    a = a_ref[...].reshape(nb, ts, POOL_W)
    ext_ref[:, POOL_PAD:POOL_PAD + ts, :] = a
    pos = pos0 + s * ts + lax.broadcasted_iota(jnp.int32, (1, ts, 1), 1)
    for g, w in enumerate(POOL_WINDOWS):
        sl = slice(g * POOL_GW, (g + 1) * POOL_GW)
        acc = a[:, :, sl]
        for k in range(1, w):
            acc = acc + ext_ref[:, POOL_PAD - k:POOL_PAD - k + ts, sl]
        cnt = jnp.minimum(pos + 1, w).astype(F32)
        m = (acc / cnt - a[:, :, sl]).reshape(nb * ts, POOL_GW)
        y_ref[:, sl] = _mm(m, wph_ref[g], wpl_ref[g], hp) * ps_ref[:, sl]
    tail = ext_ref[:, ts:ts + POOL_PAD, :]
    pn_ref[...] = tail
    ext_ref[:, 0:POOL_PAD, :] = tail


def _pool(zr, buf, wp_hl, ps, *, nseq, seqlen, pos0, hp):
    if seqlen >= 512:
        nb, ts = 1, 512
    else:
        nb, ts = _tile(nseq, 16), seqlen
    n_s = seqlen // ts
    kern = functools.partial(_pool_kernel, nb=nb, ts=ts, pos0=pos0, hp=hp)
    return pl.pallas_call(
        kern,
        grid=(nseq // nb, n_s),
        in_specs=[pl.BlockSpec((nb * ts, POOL_W), lambda b, s: (b * n_s + s, ZR_A_BLK[1])),
                  pl.BlockSpec((nb, POOL_PAD, POOL_W), lambda b, s: (b, 0, 0)),
                  _const_spec(wp_hl[0].shape), _const_spec(wp_hl[1].shape),
                  _const_spec((1, POOL_W))],
        out_specs=[pl.BlockSpec((nb * ts, POOL_W), lambda b, s: (b * n_s + s, 0)),
                   pl.BlockSpec((nb, POOL_PAD, POOL_W), lambda b, s: (b, 0, 0))],
        out_shape=[jax.ShapeDtypeStruct((nseq * seqlen, POOL_W), F32),
                   jax.ShapeDtypeStruct((nseq, POOL_PAD, POOL_W), F32)],
        scratch_shapes=[pltpu.VMEM((nb, POOL_PAD + ts, POOL_W), F32)],
        compiler_params=_cparams("parallel", "arbitrary"),
        name="pool",
    )(zr, buf, *wp_hl, ps)


def _sgu_kernel(zc_ref, g_ref, ws_ref, b_ref, y_ref, v_ref, *, n_chunk, blk, hp):
    z = jax.nn.gelu(zc_ref[...])
    r = lax.broadcasted_iota(jnp.int32, (CHUNK, CHUNK), 0)
    c = lax.broadcasted_iota(jnp.int32, (CHUNK, CHUNK), 1)
    mask = (c <= r) & ((r // blk) == (c // blk))
    for g in range(SG_GROUPS):
        sl = slice(g * CHUNK, (g + 1) * CHUNK)
        vg = _rms(z[:, SG_W + g * CHUNK:SG_W + (g + 1) * CHUNK], g_ref[g:g + 1, :])
        v_ref[:, sl] = vg
        ws = jnp.where(mask, ws_ref[g], 0.0)
        for ch in range(n_chunk):
            rows = slice(ch * CHUNK, (ch + 1) * CHUNK)
            sp = _mm2(ws, vg[rows], hp) + b_ref[:, sl]
            y_ref[rows, sl] = z[rows, sl] * sp


def _sgu(zr, g_sgv, ws, bfull, *, blk, hp):
    t = zr.shape[0]
    rows = _tile(t, 512)
    n_chunk = rows // CHUNK
    return pl.pallas_call(
        functools.partial(_sgu_kernel, n_chunk=n_chunk, blk=blk, hp=hp),
        grid=(t // rows,),
        in_specs=[pl.BlockSpec((rows, ZR_C_BLK[0]), lambda i: (i, ZR_C_BLK[1])),
                  _const_spec(g_sgv.shape),
                  _const_spec(ws.shape),
                  _const_spec(bfull.shape)],
        out_specs=[pl.BlockSpec((rows, SG_W), lambda i: (i, 0))] * 2,
        out_shape=[jax.ShapeDtypeStruct((t, SG_W), F32)] * 2,
        compiler_params=_cparams("parallel"),
        name="sgu",
    )(zr, g_sgv, ws, bfull)


def _mla_prep_kernel(qa_ref, kv_ref, cos_ref, sin_ref, gqa_ref, gkva_ref, wqh_ref, wql_ref, gqn_ref,
                     qpost_ref, wkvh_ref, wkvl_ref, gkn_ref, q_ref, c_ref, kpe_ref, *maybe_kv,
                     want_kv, hp):
    cos = cos_ref[...]
    sin = sin_ref[...]
    qn = _rms(qa_ref[...], gqa_ref[...])
    qq = _mm(qn, wqh_ref[...], wql_ref[...], hp)
    kv = kv_ref[...]
    c = _rms(kv[:, :KV_LORA], gkva_ref[...])
    c_ref[...] = c
    kpe = kv[:, KV_LORA:2 * KV_LORA] * cos + kv[:, 2 * KV_LORA:] * sin
    kpe_ref[...] = kpe
    nq = N_HEADS * HEAD_PAD

    def put_split(ref, idx, val):
        hi, lo = _split(val)
        ref[:, 2 * idx * HEAD_PAD:(2 * idx + 1) * HEAD_PAD] = hi
        ref[:, (2 * idx + 1) * HEAD_PAD:(2 * idx + 2) * HEAD_PAD] = lo

    def put_packed(ref, idx, val, is_query):
        hi = lax.bitcast_convert_type(lax.bitcast_convert_type(val, jnp.int32) & jnp.int32(-65536), F32)
        lo = val - hi
        second = lo if is_query else hi
        third = hi if is_query else lo
        packed = jnp.concatenate([hi[:, :QK_DIM], second[:, :QK_DIM], third[:, :QK_NOPE]], axis=1)
        ref[:, 2 * idx * HEAD_PAD:(2 * idx + 2) * HEAD_PAD] = packed.astype(BF16)

    if want_kv:
        k_ref, v_ref = maybe_kv
        kvx = _mm(c, wkvh_ref[...], wkvl_ref[...], hp)
        for pair in range(N_HEADS // 2):
            put_split(v_ref, pair, kvx[:, nq + pair * HEAD_PAD:nq + (pair + 1) * HEAD_PAD])
    for h in range(N_HEADS):
        sl = slice(h * HEAD_PAD, (h + 1) * HEAD_PAD)
        qh = qq[:, sl] * cos + qq[:, nq + h * HEAD_PAD:nq + (h + 1) * HEAD_PAD] * sin
        qh = qh * lax.rsqrt(jnp.sum(qh * qh, axis=-1, keepdims=True) * (1.0 / QK_DIM) + EPS)
        qh = qh * gqn_ref[:, sl] * qpost_ref[:, sl]
        if want_kv:
            put_packed(q_ref, h, qh, True)
            kh = kvx[:, sl] + kpe
            kh = kh * lax.rsqrt(jnp.sum(kh * kh, axis=-1, keepdims=True) * (1.0 / QK_DIM) + EPS)
            put_packed(k_ref, h, kh * gkn_ref[:, sl], False)
        else:
            q_ref[:, sl] = qh


def _mla_prep(zr, cos, sin, lw, qpost, *, want_kv, hp):
    t = zr.shape[0]
    tm = _tile(t, 512)
    tab_nblk = cos.shape[0] // tm
    nq = N_HEADS * HEAD_PAD
    in_specs = [pl.BlockSpec((tm, ZR_QA_BLK[0]), lambda i: (i, ZR_QA_BLK[1])),
                pl.BlockSpec((tm, ZR_KV_BLK[0]), lambda i: (i, ZR_KV_BLK[1])),
                pl.BlockSpec((tm, HEAD_PAD), lambda i: (i % tab_nblk, 0)),
                pl.BlockSpec((tm, HEAD_PAD), lambda i: (i % tab_nblk, 0)),
                _const_spec((1, Q_LORA)), _const_spec((1, KV_LORA)),
                _const_spec(lw["wq"][0].shape), _const_spec(lw["wq"][1].shape),
                _const_spec((1, nq)), _const_spec((1, nq)),
                _const_spec(lw["wkv"][0].shape), _const_spec(lw["wkv"][1].shape),
                _const_spec((1, nq))]
    args = [zr, zr, cos, sin, lw["g_qa"], lw["g_kva"], *lw["wq"], lw["g_qn"], qpost, *lw["wkv"], lw["g_kn"]]
    q_w, q_dt = (2 * nq, BF16) if want_kv else (nq, F32)
    out_specs = [pl.BlockSpec((tm, q_w), lambda i: (i, 0)),
                 pl.BlockSpec((tm, KV_LORA), lambda i: (i, 0)),
                 pl.BlockSpec((tm, HEAD_PAD), lambda i: (i, 0))]
    out_shape = [jax.ShapeDtypeStruct((t, q_w), q_dt),
                 jax.ShapeDtypeStruct((t, KV_LORA), F32),
                 jax.ShapeDtypeStruct((t, HEAD_PAD), F32)]
    if want_kv:
        out_specs += [pl.BlockSpec((tm, 2 * nq), lambda i: (i, 0)),
                      pl.BlockSpec((tm, nq), lambda i: (i, 0))]
        out_shape += [jax.ShapeDtypeStruct((t, 2 * nq), BF16),
                      jax.ShapeDtypeStruct((t, nq), BF16)]
    return pl.pallas_call(
        functools.partial(_mla_prep_kernel, want_kv=want_kv, hp=hp),
        grid=(t // tm,),
        in_specs=in_specs, out_specs=out_specs, out_shape=out_shape,
        compiler_params=_cparams("parallel"),
        name="mla_prep",
    )(*args)


def _mla_prompt_kernel(qi_ref, ki_ref, q_ref, k_ref, v_ref, o_ref, m_s, l_s, acc_s, *, tq, sub):
    t = pl.program_id(2)
    qi = qi_ref[t]
    ki = ki_ref[t]

    @pl.when(ki == 0)
    def _():
        m_s[...] = jnp.full(m_s.shape, -jnp.inf, F32)
        l_s[...] = jnp.zeros(l_s.shape, F32)
        acc_s[...] = jnp.zeros(acc_s.shape, F32)

    def step(diagonal):
        for j in range(2):
            base = 2 * j * HEAD_PAD
            for r in range(tq // sub):
                rows = slice(r * sub, (r + 1) * sub)
                n_k = (r + 1) * sub if diagonal else tq
                vv = v_ref[0:n_k, :]
                s = lax.dot_general(q_ref[rows, base:base + 2 * HEAD_PAD], k_ref[0:n_k, base:base + 2 * HEAD_PAD],
                                    _NT, preferred_element_type=F32)
                if diagonal:
                    row = r * sub + lax.broadcasted_iota(jnp.int32, (sub, n_k), 0)
                    col = lax.broadcasted_iota(jnp.int32, (sub, n_k), 1)
                    s = jnp.where(col <= row, s, -jnp.inf)
                m_prev = m_s[j, rows]
                m_new = jnp.maximum(m_prev, jnp.max(s, axis=-1, keepdims=True))
                alpha = jnp.exp(m_prev - m_new)
                p = jnp.exp(s - m_new)
                l_s[j, rows] = alpha * l_s[j, rows] + jnp.sum(p, axis=-1, keepdims=True)
                pv = jnp.dot(p.astype(BF16), vv, preferred_element_type=F32)
                acc_s[j, rows] = alpha * acc_s[j, rows] + (pv[:, :HEAD_PAD] + pv[:, HEAD_PAD:])
                m_s[j, rows] = m_new

    @pl.when(ki < qi)
    def _():
        step(False)

    @pl.when(ki == qi)
    def _():
        step(True)
        lane = lax.broadcasted_iota(jnp.int32, (tq, 2 * V_HEAD), 1)
        o_ref[...] = jnp.where(lane < V_HEAD, acc_s[0] / l_s[0], acc_s[1] / l_s[1])


def _mla_prompt(q2, k2, v2, *, nseq, seqlen):
    tq = _tile(seqlen, 1024)
    sub = _tile(tq, 256)
    nq = seqlen // tq
    qi_np = np.array([i for i in range(nq) for _ in range(i + 1)], np.int32)
    ki_np = np.array([j for i in range(nq) for j in range(i + 1)], np.int32)
    n_tri = int(qi_np.shape[0])
    pair_w = 4 * HEAD_PAD
    grid_spec = pltpu.PrefetchScalarGridSpec(
        num_scalar_prefetch=2,
        grid=(nseq, N_HEADS // 2, n_tri),
        in_specs=[pl.BlockSpec((tq, pair_w), lambda b, hp_, t, qi, ki: (b * nq + qi[t], hp_)),
                  pl.BlockSpec((tq, pair_w), lambda b, hp_, t, qi, ki: (b * nq + ki[t], hp_)),
                  pl.BlockSpec((tq, 2 * HEAD_PAD), lambda b, hp_, t, qi, ki: (b * nq + ki[t], hp_))],
        out_specs=pl.BlockSpec((tq, 2 * V_HEAD), lambda b, hp_, t, qi, ki: (b * nq + qi[t], hp_)),
        scratch_shapes=[pltpu.VMEM((2, tq, 1), F32), pltpu.VMEM((2, tq, 1), F32),
                        pltpu.VMEM((2, tq, 2 * V_HEAD), F32)])
    return pl.pallas_call(
        functools.partial(_mla_prompt_kernel, tq=tq, sub=sub),
        grid_spec=grid_spec,
        out_shape=jax.ShapeDtypeStruct((nseq * seqlen, N_HEADS * V_HEAD), F32),
        compiler_params=_cparams("parallel", "parallel", "arbitrary"),
        name="mla_prompt",
    )(jnp.asarray(qi_np), jnp.asarray(ki_np), q2, k2, v2)


def _mla_sample_kernel(pt_ref, q_ref, ckv_hbm, kpe_hbm, cn_ref, kn_ref, wkt_ref, wktp_ref, fold_ref, wv_ref,
                       o_ref, lhs_s, qr_s, m_s, l_s, ctx_s, cbuf, kbuf, sem, *, layer, pps, cpp, n_new):
    b = pl.program_id(0)
    s_idx = pl.program_id(1)
    n_steps = pl.num_programs(1)
    n_rows = N_HEADS * n_new
    g = b * n_steps + s_idx
    slot = g % 2

    def page_copies(bb, ss, sl):
        cps = []
        for i in range(pps):
            page = pt_ref[bb, ss * pps + i]
            cps.append(pltpu.make_async_copy(ckv_hbm.at[layer, page],
                                             cbuf.at[sl, pl.ds(i * PAGE, PAGE), :], sem.at[0, sl]))
            cps.append(pltpu.make_async_copy(kpe_hbm.at[layer, page],
                                             kbuf.at[sl, :, pl.ds(i * PAGE, PAGE)], sem.at[1, sl]))
        return cps

    @pl.when(g == 0)
    def _():
        for cp in page_copies(b, s_idx, slot):
            cp.start()

    @pl.when(g + 1 < pl.num_programs(0) * n_steps)
    def _():
        for cp in page_copies((g + 1) // n_steps, (g + 1) % n_steps, 1 - slot):
            cp.start()

    for cp in page_copies(b, s_idx, slot):
        cp.wait()

    @pl.when(s_idx == 0)
    def _():
        q = q_ref[...]
        lane_head = lax.broadcasted_iota(jnp.int32, q.shape, 1) // HEAD_PAD
        qblk = jnp.concatenate([jnp.where(lane_head == h, q, 0.0) for h in range(N_HEADS)], axis=0)
        qblk = qblk.astype(BF16)
        lhs_s[0:N_HEADS * QK_NOPE, :] = wkt_ref[...]
        lhs_s[N_HEADS * QK_NOPE:, :] = jnp.dot(qblk, wktp_ref[...], preferred_element_type=F32).astype(BF16)
        qr_s[...] = jnp.dot(qblk, fold_ref[...], preferred_element_type=F32).astype(BF16)
        m_s[...] = jnp.full(m_s.shape, -jnp.inf, F32)
        l_s[...] = jnp.zeros(l_s.shape, F32)
        ctx_s[...] = jnp.zeros(ctx_s.shape, F32)

    def scores(c, sr, ssr):
        cb = c.astype(BF16)
        big = lax.dot_general(lhs_s[...], cb, _NT, preferred_element_type=F32)
        pieces = []
        for h in range(N_HEADS):
            kn = big[h * QK_NOPE:(h + 1) * QK_NOPE]
            ss = jnp.sum(kn * kn, axis=0, keepdims=True) + ssr
            rk = lax.rsqrt(ss * (1.0 / QK_DIM) + EPS)
            rows = slice(N_HEADS * QK_NOPE + h * n_new, N_HEADS * QK_NOPE + (h + 1) * n_new)
            pieces.append((big[rows] + sr[h * n_new:(h + 1) * n_new]) * rk)
        return cb, jnp.concatenate(pieces, axis=0)

    def update(chunks):
        m_new = m_s[...]
        for _, sc in chunks:
            m_new = jnp.maximum(m_new, jnp.max(sc, axis=-1, keepdims=True))
        alpha = jnp.exp(m_s[...] - m_new)
        l_new = alpha * l_s[...]
        ctx = alpha * ctx_s[...]
        for cb, sc in chunks:
            p = jnp.exp(sc - m_new)
            l_new = l_new + jnp.sum(p, axis=-1, keepdims=True)
            ctx = ctx + jnp.dot(p.astype(BF16), cb, preferred_element_type=F32)
        l_s[...] = l_new
        ctx_s[...] = ctx
        m_s[...] = m_new

    chunks = []
    for j in range(0, pps, cpp):
        keys = pl.ds(j * PAGE, cpp * PAGE)
        kpe_t = kbuf[slot, :, keys]
        chunks.append(scores(cbuf[slot, keys, :],
                             jnp.dot(qr_s[...], kpe_t.astype(BF16), preferred_element_type=F32),
                             jnp.sum(kpe_t * kpe_t, axis=0, keepdims=True)))
    update(chunks)

    @pl.when(s_idx == pl.num_programs(1) - 1)
    def _():
        cpad = jnp.concatenate([cn_ref[...], jnp.zeros((PAGE - n_new, KV_LORA), F32)], axis=0)
        kpad = jnp.concatenate([kn_ref[...], jnp.zeros((PAGE - n_new, QK_ROPE), F32)], axis=0)
        key = lax.broadcasted_iota(jnp.int32, (n_rows, PAGE), 1)
        qry = lax.broadcasted_iota(jnp.int32, (n_rows, PAGE), 0) % n_new
        sr = lax.dot_general(qr_s[...], kpad.astype(BF16), _NT, preferred_element_type=F32)
        ones = jnp.ones((8, QK_ROPE), BF16)
        ssr = lax.dot_general(ones, (kpad * kpad).astype(BF16), _NT, preferred_element_type=F32)[0:1]
        cb, sc = scores(cpad, sr, ssr)
        update([(cb, jnp.where(key <= qry, sc, -jnp.inf))])
        ctx = (ctx_s[...] / l_s[...]).astype(BF16)
        res = jnp.dot(ctx, wv_ref[...], preferred_element_type=F32)
        lane_head = lax.broadcasted_iota(jnp.int32, (n_new, N_HEADS * V_HEAD), 1) // V_HEAD
        out = jnp.zeros((n_new, N_HEADS * V_HEAD), F32)
        for h in range(N_HEADS):
            out = out + jnp.where(lane_head == h, res[h * n_new:(h + 1) * n_new], 0.0)
        o_ref[...] = out


def _mla_sample(q, c_new, kpe_new, cache_ckv, cache_kpe_t, page_table, layer, lw):
    nseq, n_new, _ = q.shape
    n_pages = page_table.shape[1]
    pps = _tile(n_pages, 64)
    n_steps = n_pages // pps
    n_rows = N_HEADS * n_new

    in_specs = [pl.BlockSpec((None, n_new, N_HEADS * HEAD_PAD), lambda b, s, pt: (b, 0, 0)),
                pl.BlockSpec(memory_space=pl.ANY), pl.BlockSpec(memory_space=pl.ANY),
                pl.BlockSpec((None, n_new, KV_LORA), lambda b, s, pt: (b, 0, 0)),
                pl.BlockSpec((None, n_new, QK_ROPE), lambda b, s, pt: (b, 0, 0)),
                _const_spec(lw["wkt"].shape), _const_spec(lw["wktp"].shape),
                _const_spec(lw["fold"].shape), _const_spec(lw["wv"].shape)]
    grid_spec = pltpu.PrefetchScalarGridSpec(
        num_scalar_prefetch=1,
        grid=(nseq, n_steps),
        in_specs=in_specs,
        out_specs=pl.BlockSpec((None, n_new, N_HEADS * V_HEAD), lambda b, s, pt: (b, 0, 0)),
        scratch_shapes=[pltpu.VMEM((N_HEADS * QK_NOPE + n_rows, KV_LORA), BF16),
                        pltpu.VMEM((n_rows, QK_ROPE), BF16),
                        pltpu.VMEM((n_rows, 1), F32), pltpu.VMEM((n_rows, 1), F32),
                        pltpu.VMEM((n_rows, KV_LORA), F32),
                        pltpu.VMEM((2, pps * PAGE, KV_LORA), F32),
                        pltpu.VMEM((2, QK_ROPE, pps * PAGE), F32),
                        pltpu.SemaphoreType.DMA((2, 2))])
    return pl.pallas_call(
        functools.partial(_mla_sample_kernel, layer=layer, pps=pps, cpp=_tile(pps, 4), n_new=n_new),
        grid_spec=grid_spec,
        out_shape=jax.ShapeDtypeStruct((nseq, n_new, N_HEADS * V_HEAD), F32),
        compiler_params=_cparams("arbitrary", "arbitrary"),
        name="mla_sample",
    )(page_table, q, cache_ckv, cache_kpe_t, c_new, kpe_new,
      lw["wkt"], lw["wktp"], lw["fold"], lw["wv"])


def _memkv_kernel(x_ref, g_ref, wh_ref, wl_ref, gk_ref, k_ref, v_ref, *, hp):
    kv = _mm(_rms(x_ref[...], g_ref[...]), wh_ref[...], wl_ref[...], hp)
    for h in range(MEM_HEADS):
        sl = slice(h * MEM_HD, (h + 1) * MEM_HD)
        k_ref[:, sl] = _rms(kv[:, sl], gk_ref[...])
    v_ref[...] = kv[:, MEM_W:]


def _memkv(mem, g_mem, wmkv_hl, g_mkn, *, hp):
    t = mem.shape[0]
    tm = _tile(t, 256)
    return pl.pallas_call(
        functools.partial(_memkv_kernel, hp=hp),
        grid=(t // tm,),
        in_specs=[pl.BlockSpec((tm, D_MODEL), lambda i: (i, 0)),
                  _const_spec((1, D_MODEL)), _const_spec(wmkv_hl[0].shape), _const_spec(wmkv_hl[1].shape),
                  _const_spec((1, MEM_HD))],
        out_specs=[pl.BlockSpec((tm, MEM_W), lambda i: (i, 0))] * 2,
        out_shape=[jax.ShapeDtypeStruct((t, MEM_W), F32)] * 2,
        compiler_params=_cparams("parallel"),
        name="memkv",
    )(mem, g_mem, *wmkv_hl, g_mkn)


def _memattn_kernel(q_ref, k_ref, v_ref, g_ref, o_ref, *, hp, nb, ts, rows_by_head):
    scale = 1.0 / math.sqrt(MEM_HD)
    for i in range(nb):
        rows = slice(i * ts, (i + 1) * ts)
        for h in range(MEM_HEADS):
            sl = slice(h * MEM_HD, (h + 1) * MEM_HD)
            if rows_by_head:
                k = k_ref[i, pl.ds(h, N_MEM, stride=MEM_HEADS), :]
                v = v_ref[i, pl.ds(h, N_MEM, stride=MEM_HEADS), :]
            else:
                k = k_ref[:, sl]
                v = v_ref[:, sl]
            q = _rms(q_ref[rows, sl], g_ref[...]) * scale
            s = _mm2(q, k, hp, _NT)
            p = jnp.exp(s - jnp.max(s, axis=-1, keepdims=True))
            l = jnp.sum(p, axis=-1, keepdims=True)
            o_ref[rows, sl] = _mm2(p, v, hp) / l


def _memattn(zr, mk, mv, g_mqn, *, nseq, seqlen, kv_index, hp):
    rows_by_head = mk.shape[-1] == MEM_HD
    if rows_by_head:
        nb, ts, nq = _tile(nseq, 8), seqlen, 1
        kv_blk = (None,) * (mk.ndim - 3) + (nb, N_MEM * MEM_HEADS, MEM_HD)
    else:
        nb, ts = 1, _tile(seqlen, 512)
        nq = seqlen // ts
        kv_blk = (N_MEM, MEM_W)
    return pl.pallas_call(
        functools.partial(_memattn_kernel, hp=hp, nb=nb, ts=ts, rows_by_head=rows_by_head),
        grid=(nseq // nb, nq),
        in_specs=[pl.BlockSpec((nb * ts, ZR_MQ_BLK[0]), lambda b, i: (b * nq + i, ZR_MQ_BLK[1])),
                  pl.BlockSpec(kv_blk, lambda b, i: kv_index(b)),
                  pl.BlockSpec(kv_blk, lambda b, i: kv_index(b)),
                  _const_spec((1, MEM_HD))],
        out_specs=pl.BlockSpec((nb * ts, MEM_W), lambda b, i: (b * nq + i, 0)),
        out_shape=jax.ShapeDtypeStruct((nseq * seqlen, MEM_W), F32),
        compiler_params=_cparams("parallel", "parallel"),
        name="memattn",
    )(zr, mk, mv, g_mqn)


def _merge_kernel(x_ref, ya_ref, yb_ref, yc_ref, ym_ref, zg_ref, wbrh_ref, wbrl_ref, woh_ref, wol_ref,
                  gf_ref, wrh_ref, wrl_ref, br_ref, x1_ref, h2_ref, cw_ref, *, hp):
    acc = None
    for b, y_ref in enumerate((ya_ref, yb_ref, yc_ref, ym_ref)):
        term = zg_ref[:, b * D_MODEL:(b + 1) * D_MODEL] * _mm(y_ref[...], wbrh_ref[b], wbrl_ref[b], hp)
        acc = term if acc is None else acc + term
    x1 = x_ref[...] + _mm(acc, woh_ref[...], wol_ref[...], hp)
    x1_ref[...] = x1
    h2 = _rms(x1, gf_ref[...])
    h2_ref[...] = h2
    logits = _mm(h2, wrh_ref[...], wrl_ref[...], True) + br_ref[...]
    lane = lax.broadcasted_iota(jnp.int32, logits.shape, 1).astype(F32)
    big = 1e9
    is_grp = (lane >= N_EXPERTS) & (lane < N_EXPERTS + N_GROUPS)
    gl = jnp.where(is_grp, logits, -jnp.inf)
    ge = jnp.exp(gl - jnp.max(gl, axis=-1, keepdims=True))
    gprob = ge / jnp.sum(ge, axis=-1, keepdims=True)
    gp = jnp.max(gprob, axis=-1, keepdims=True)
    gi = jnp.min(jnp.where(is_grp & (gprob == gp), lane, big), axis=-1, keepdims=True) - N_EXPERTS
    sel = (lane >= gi * EXP_PER_GROUP) & (lane < (gi + 1) * EXP_PER_GROUP)
    el = jnp.where(sel, logits, -jnp.inf)
    ee = jnp.exp(el - jnp.max(el, axis=-1, keepdims=True))
    eprob = ee / jnp.sum(ee, axis=-1, keepdims=True)
    v1 = jnp.max(eprob, axis=-1, keepdims=True)
    i1 = jnp.min(jnp.where(sel & (eprob == v1), lane, big), axis=-1, keepdims=True)
    rest = jnp.where(sel & (lane != i1), eprob, -1.0)
    v2 = jnp.max(rest, axis=-1, keepdims=True)
    i2 = jnp.min(jnp.where(rest == v2, lane, big), axis=-1, keepdims=True)
    den = v1 + v2
    cw = jnp.where(lane == i1, gp * v1 / den, jnp.where(lane == i2, gp * v2 / den, 0.0))
    cw_ref[...] = jnp.where(lane == GROUP_LANE, gi, cw)


def _merge(x, ya, yb, yc, ym, zg, lw, *, hp):
    t = x.shape[0]
    tm = _tile(t, 512)
    row = lambda w: pl.BlockSpec((tm, w), lambda i: (i, 0))
    consts = [*lw["wbr"], *lw["wo"], lw["g_ffn"], *lw["wr"], lw["b_r"]]
    return pl.pallas_call(
        functools.partial(_merge_kernel, hp=hp),
        grid=(t // tm,),
        in_specs=[row(D_MODEL), row(POOL_W), row(512), row(SG_W), row(MEM_W), row(N_BRANCH * D_MODEL)]
                 + [_const_spec(c.shape) for c in consts],
        out_specs=[row(D_MODEL), row(D_MODEL), row(128)],
        out_shape=[jax.ShapeDtypeStruct((t, D_MODEL), F32),
                   jax.ShapeDtypeStruct((t, D_MODEL), F32),
                   jax.ShapeDtypeStruct((t, 128), F32)],
        compiler_params=_cparams("parallel"),
        name="merge",
    )(x, ya, yb, yc, ym, zg, *consts)


def _moe_kernel(h_ref, cw_ref, x_ref, wguh_ref, wgul_ref, wdh_ref, wdl_ref, o_ref, acc_s, hh_s, hl_s, *, hp):
    e = pl.program_id(1)

    @pl.when(e == 0)
    def _():
        acc_s[...] = jnp.zeros(acc_s.shape, F32)
        hh_s[...], hl_s[...] = _split(h_ref[...])

    dot = lambda a, b: jnp.dot(a, b, preferred_element_type=F32)
    ab = dot(hh_s[...], wguh_ref[...])
    if hp:
        ab = ab + (dot(hl_s[...], wguh_ref[...]) + dot(hh_s[...], wgul_ref[...]))
    cw = cw_ref[...]
    lane = lax.broadcasted_iota(jnp.int32, cw.shape, 1)
    cwe = jnp.sum(jnp.where(lane == e, cw, 0.0), axis=-1, keepdims=True)
    t = jax.nn.silu(ab[:, :D_EXPERT]) * ab[:, D_EXPERT:] * cwe
    acc_s[...] += _mm(t, wdh_ref[...], wdl_ref[...], hp)

    @pl.when(e == N_EXPERTS - 1)
    def _():
        o_ref[...] = x_ref[...] + acc_s[...]


def _moe(h2, cw, x1, wgu_hl, wd_hl, *, hp):
    t = h2.shape[0]
    tm = _tile(t, 512)
    wgu_spec = pl.BlockSpec((None, D_MODEL, 2 * D_EXPERT), lambda i, e: (e, 0, 0))
    wd_spec = pl.BlockSpec((None, D_EXPERT, D_MODEL), lambda i, e: (e, 0, 0))
    return pl.pallas_call(
        functools.partial(_moe_kernel, hp=hp),
        grid=(t // tm, N_EXPERTS),
        in_specs=[pl.BlockSpec((tm, D_MODEL), lambda i, e: (i, 0)),
                  pl.BlockSpec((tm, 128), lambda i, e: (i, 0)),
                  pl.BlockSpec((tm, D_MODEL), lambda i, e: (i, 0)),
                  wgu_spec, wgu_spec, wd_spec, wd_spec],
        out_specs=pl.BlockSpec((tm, D_MODEL), lambda i, e: (i, 0)),
        out_shape=jax.ShapeDtypeStruct((t, D_MODEL), F32),
        scratch_shapes=[pltpu.VMEM((tm, D_MODEL), F32), pltpu.VMEM((tm, D_MODEL), BF16),
                        pltpu.VMEM((tm, D_MODEL), BF16)],
        compiler_params=_cparams("parallel", "arbitrary"),
        name="moe",
    )(h2, cw, x1, *wgu_hl, *wd_hl)


def _split3(a):
    mask = lambda v: lax.bitcast_convert_type(lax.bitcast_convert_type(v, jnp.int32) & jnp.int32(-65536), F32)
    p1 = mask(a)
    r1 = a - p1
    p2 = mask(r1)
    return p1.astype(BF16), p2.astype(BF16), (r1 - p2).astype(BF16)


def _moe_sorted_kernel(h_ref, cw_ref, x_ref, wguh_ref, wgul_ref, wdh_ref, wdl_ref, o_ref,
                       xh_s, xl_s, cws_s, ys_s, pt_s, seg_s, *, tb, ch, eps):
    step = pl.program_id(1)
    e0 = step * eps
    dot = lambda a, b: jnp.dot(a, b, preferred_element_type=F32)

    @pl.when(step == 0)
    def _():
        cw = cw_ref[...]
        lane = lax.broadcasted_iota(jnp.int32, (tb, 128), 1)
        gi = jnp.sum(jnp.where(lane == GROUP_LANE, cw, 0.0), axis=-1, keepdims=True)
        oh = jnp.where(lane.astype(F32) == gi, 1.0, 0.0)
        r = lax.broadcasted_iota(jnp.int32, (tb, tb), 0)
        c = lax.broadcasted_iota(jnp.int32, (tb, tb), 1)
        lower = jnp.where(c < r, 1.0, 0.0).astype(BF16)
        rank = dot(lower, oh.astype(BF16))
        cnt = jnp.sum(oh, axis=0, keepdims=True)
        lane_row = lax.broadcasted_iota(jnp.int32, (1, 128), 1)
        off = jnp.zeros((1, 128), F32)
        for g in range(N_GROUPS - 1):
            off = off + jnp.where(lane_row > g, cnt[:, g:g + 1], 0.0)
        slot = jnp.sum(oh * (off + rank), axis=-1, keepdims=True)
        pt_s[...] = jnp.where(slot == c.astype(F32), 1.0, 0.0).astype(BF16)
        d_hi = jnp.floor(slot * (1.0 / 32.0))
        digits = jnp.where(lane == 0, d_hi, jnp.where(lane == 1, slot - 32.0 * d_hi, 0.0)).astype(BF16)
        er = lax.broadcasted_iota(jnp.int32, (128, 128), 0)
        ec = lax.broadcasted_iota(jnp.int32, (128, 128), 1)
        eye = jnp.where(er == ec, 1.0, 0.0).astype(BF16)
        dt = lax.dot_general(eye, digits, _NT, preferred_element_type=F32)
        slot_row = 32.0 * dt[0:1] + dt[1:2]
        p = jnp.where(slot_row == r.astype(F32), 1.0, 0.0).astype(BF16)
        h_hi, h_lo = _split(h_ref[...])
        xh_s[...] = dot(p, h_hi).astype(BF16)
        xl_s[...] = dot(p, h_lo).astype(BF16)
        c1, c2, c3 = _split3(cw)
        cws_s[...] = dot(p, c1) + (dot(p, c2) + dot(p, c3))
        ys_s[...] = jnp.zeros(ys_s.shape, F32)
        for g in range(N_GROUPS):
            seg_s[g] = off[0, g].astype(jnp.int32)
            seg_s[N_GROUPS + g] = (off[0, g] + cnt[0, g]).astype(jnp.int32)

    grp = e0 // EXP_PER_GROUP
    start = seg_s[grp]
    end = seg_s[N_GROUPS + grp]
    w0 = (start // 16) * 16
    n_win = jnp.where(end > start, (end - w0 + ch - 1) // ch, 0)

    def window(j, carry):
        lo = w0 + j * ch
        ws = pl.multiple_of(jnp.minimum(lo, tb - ch), 16)
        rows = pl.ds(ws, ch)
        xh = xh_s[rows, :]
        xl = xl_s[rows, :]
        cws = cws_s[rows, :]
        lane = lax.broadcasted_iota(jnp.int32, cws.shape, 1)
        row = ws + lax.broadcasted_iota(jnp.int32, (ch, 1), 0)
        y = None
        for j in range(eps):
            ab = dot(xh, wguh_ref[j]) + (dot(xl, wguh_ref[j]) + dot(xh, wgul_ref[j]))
            cwe = jnp.sum(jnp.where(lane == e0 + j, cws, 0.0), axis=-1, keepdims=True)
            cwe = jnp.where(row >= lo, cwe, 0.0)
            t = jax.nn.silu(ab[:, :D_EXPERT]) * ab[:, D_EXPERT:] * cwe
            yj = _mm(t, wdh_ref[j], wdl_ref[j], True)
            y = yj if y is None else y + yj
        ys_s[rows, :] += y
        return carry

    lax.fori_loop(0, n_win, window, 0)

    @pl.when(step == pl.num_programs(1) - 1)
    def _():
        y_hi, y_lo = _split(ys_s[...])
        pt = pt_s[...]
        o_ref[...] = x_ref[...] + (dot(pt, y_hi) + dot(pt, y_lo))


def _moe_sorted(h2, cw, x1, wgu_hl, wd_hl):
    t = h2.shape[0]
    tb = 1024
    once = lambda w: pl.BlockSpec((tb, w), lambda i, e: (i, 0), pipeline_mode=pl.Buffered(1))
    eps = 2
    wgu_spec = pl.BlockSpec((eps, D_MODEL, 2 * D_EXPERT), lambda i, e: (e, 0, 0))
    wd_spec = pl.BlockSpec((eps, D_EXPERT, D_MODEL), lambda i, e: (e, 0, 0))
    return pl.pallas_call(
        functools.partial(_moe_sorted_kernel, tb=tb, ch=320, eps=eps),
        grid=(t // tb, N_EXPERTS // eps),
        in_specs=[once(D_MODEL), once(128), once(D_MODEL), wgu_spec, wgu_spec, wd_spec, wd_spec],
        out_specs=pl.BlockSpec((tb, D_MODEL), lambda i, e: (i, 0)),
        out_shape=jax.ShapeDtypeStruct((t, D_MODEL), F32),
        scratch_shapes=[pltpu.VMEM((tb, D_MODEL), BF16), pltpu.VMEM((tb, D_MODEL), BF16),
                        pltpu.VMEM((tb, 128), F32), pltpu.VMEM((tb, D_MODEL), F32),
                        pltpu.VMEM((tb, tb), BF16), pltpu.SMEM((2 * N_GROUPS,), jnp.int32)],
        compiler_params=_cparams("parallel", "arbitrary"),
        name="moe_sorted",
    )(h2, cw, x1, *wgu_hl, *wd_hl)


def _hl(w):
    b = lax.bitcast_convert_type(w, jnp.int32)
    b = (b + jnp.int32(0x7FFF) + ((b >> 16) & 1)) & jnp.int32(-65536)
    hi = lax.bitcast_convert_type(b, F32)
    return hi.astype(BF16), (w - hi).astype(BF16)


def _rope_tables(pos):
    half = QK_ROPE // 2
    inv = ROPE_THETA ** (-jnp.arange(half, dtype=F32) / half)
    ang = pos.astype(F32)[:, None] * inv
    cos, sin = jnp.cos(ang), jnp.sin(ang)
    n = pos.shape[0]
    cos128 = jnp.concatenate([jnp.ones((n, QK_NOPE), F32), cos, cos, jnp.zeros((n, 32), F32)], 1)
    sin128 = jnp.concatenate([jnp.zeros((n, QK_NOPE), F32), sin, sin, jnp.zeros((n, 32), F32)], 1)
    return cos128, sin128


def _fold_matrix():
    f = np.zeros((N_HEADS * HEAD_PAD, QK_ROPE), np.float32)
    for h in range(N_HEADS):
        for j in range(QK_ROPE):
            f[h * HEAD_PAD + QK_NOPE + j, j] = 1.0
    return jnp.asarray(f, BF16)


def _head_pad_gain(g):
    return jnp.tile(jnp.concatenate([g, jnp.zeros((HEAD_PAD - QK_DIM,), F32)]), N_HEADS)[None]


def _layer_weights(l, w):
    half = QK_ROPE // 2
    wi = w["w_in"][l]
    w_a, w_qa = wi[:, 0:512], wi[:, 512:768]
    w_lat, w_rope = wi[:, 768:896], wi[:, 896:928]
    w_c, w_mq, w_g = wi[:, 928:1952], wi[:, 1952:2464], wi[:, 2464:]
    z64 = jnp.zeros((D_MODEL, QK_NOPE), F32)
    z32 = jnp.zeros((D_MODEL, 32), F32)
    rope128 = jnp.concatenate([z64, w_rope, z32], 1)
    ropesw128 = jnp.concatenate([z64, -w_rope[:, half:], w_rope[:, :half], z32], 1)
    lw = {}
    lw["wr_in"] = _hl(jnp.concatenate([w_c, w_a, w_mq, w_qa, w_lat, rope128, ropesw128], 1))
    lw["wg_in"] = _hl(w_g)
    lw["g_mix"] = w["g_mix"][l][None]

    wq = w["w_qb"][l].reshape(Q_LORA, N_HEADS, QK_DIM)
    nope, x1, x2 = wq[..., :QK_NOPE], wq[..., QK_NOPE:QK_NOPE + half], wq[..., QK_NOPE + half:]
    zq = lambda n: jnp.zeros((Q_LORA, N_HEADS, n), F32)
    wq1 = jnp.concatenate([nope, x1, x2, zq(32)], -1).reshape(Q_LORA, N_HEADS * HEAD_PAD)
    wq2 = jnp.concatenate([zq(QK_NOPE), -x2, x1, zq(32)], -1).reshape(Q_LORA, N_HEADS * HEAD_PAD)
    lw["wq"] = _hl(jnp.concatenate([wq1, wq2], 1))
    lw["g_qa"] = w["g_qa"][l][None]
    lw["g_kva"] = w["g_kva"][l][None]
    lw["g_qn"] = _head_pad_gain(w["g_qn"][l])
    lw["g_kn"] = _head_pad_gain(w["g_kn"][l])

    wkv = w["w_kvb"][l].reshape(KV_LORA, N_HEADS, QK_NOPE + V_HEAD)
    kn, vv = wkv[..., :QK_NOPE], wkv[..., QK_NOPE:]
    wk_pad = jnp.concatenate([kn, jnp.zeros((KV_LORA, N_HEADS, HEAD_PAD - QK_NOPE), F32)], -1)
    wk_pad = wk_pad.reshape(KV_LORA, N_HEADS * HEAD_PAD)
    wv = vv.reshape(KV_LORA, N_HEADS * V_HEAD)
    lw["wkv"] = _hl(jnp.concatenate([wk_pad, wv], 1))
    lw["wkt"] = kn.reshape(KV_LORA, N_HEADS * QK_NOPE).T.astype(BF16)
    lw["wktp"] = wk_pad.T.astype(BF16)
    lw["wv"] = wv.astype(BF16)
    lw["fold"] = _fold_matrix()

    lw["w_pool"] = _hl(w["w_pool"][l])
    lw["pool_scale"] = w["pool_scale"][l][None]
    lw["g_sgv"] = w["g_sgv"][l]
    lw["w_sp"] = w["w_sp"][l]
    lw["b_sp"] = w["b_sp"][l]
    lw["g_mem"] = w["g_mem"][l][None]
    lw["wmkv"] = _hl(jnp.concatenate([w["w_mk"][l], w["w_mv"][l]], 1))
    lw["g_mqn"] = w["g_mqn"][l][None]
    lw["g_mkn"] = w["g_mkn"][l][None]
    lw["wbr"] = _hl(w["w_br"][l])
    lw["wo"] = _hl(w["w_o"][l])
    lw["g_ffn"] = w["g_ffn"][l][None]
    lw["wr"] = _hl(jnp.concatenate([w["w_re"][l], w["w_rg"][l],
                                    jnp.zeros((D_MODEL, 128 - N_EXPERTS - N_GROUPS), F32)], 1))
    lw["b_r"] = jnp.concatenate([w["b_re"][l], w["b_rg"][l],
                                 jnp.zeros((128 - N_EXPERTS - N_GROUPS,), F32)])[None]
    lw["wgu"] = _hl(jnp.concatenate([w["w_eg"][l], w["w_eu"][l]], -1))
    lw["wd"] = _hl(w["w_ed"][l])
    return lw


def _sgu_operands(lw, seqlen):
    blk = min(seqlen, CHUNK)
    rep = CHUNK // blk
    ws = jnp.tile(lw["w_sp"][:, :blk, :blk], (1, rep, rep))
    b = jnp.tile(lw["b_sp"][:, :blk], (1, rep))
    bfull = jnp.repeat(b.T, CHUNK, axis=1)
    return ws, bfull, blk


def _group_layer(x, lw, *, nseq, seqlen, pos0, cos, sin, pool_buf, mem_k, mem_v, kv_index,
                 mla_sample_fn, hp):
    zr = _inproj(x, lw["g_mix"], lw["wr_in"], hp=hp, sigmoid=False)
    zg = _inproj(x, lw["g_mix"], lw["wg_in"], hp=hp, sigmoid=True)
    ya, pool_new = _pool(zr, pool_buf, lw["w_pool"], lw["pool_scale"],
                         nseq=nseq, seqlen=seqlen, pos0=pos0, hp=hp)
    ws, bfull, blk = _sgu_operands(lw, seqlen)
    yc, v_rows = _sgu(zr, lw["g_sgv"], ws, bfull, blk=blk, hp=hp)
    scale = 1.0 / math.sqrt(QK_DIM)
    if mla_sample_fn is not None:
        q, c, kpe128 = _mla_prep(zr, cos, sin, lw, lw["g_kn"] * scale, want_kv=False, hp=hp)
        kpe = kpe128[:, QK_NOPE:QK_DIM]
        yb = mla_sample_fn(q, c, kpe)
    else:
        qpost = jnp.full((1, N_HEADS * HEAD_PAD), scale, F32)
        q, c, kpe128, k, v = _mla_prep(zr, cos, sin, lw, qpost, want_kv=True, hp=hp)
        kpe = kpe128[:, QK_NOPE:QK_DIM]
        yb = _mla_prompt(q, k, v, nseq=nseq, seqlen=seqlen)
    ym = _memattn(zr, mem_k, mem_v, lw["g_mqn"], nseq=nseq, seqlen=seqlen, kv_index=kv_index, hp=hp)
    x1, h2, cw = _merge(x, ya, yb, yc, ym, zg, lw, hp=hp)
    if hp and x.shape[0] % 1024 == 0:
        x2 = _moe_sorted(h2, cw, x1, lw["wgu"], lw["wd"])
    else:
        x2 = _moe(h2, cw, x1, lw["wgu"], lw["wd"], hp=hp)
    return x2, c, kpe, pool_new, v_rows


def kernel(x_prompt, x_sample, mem_prompt, cache_ckv, cache_kpe, cache_memk, cache_memv, state_pool, page_table, g_mix, w_in, g_qa, w_qb, g_kva, w_kvb, g_qn, g_kn, w_pool, pool_scale, g_sgv, w_sp, b_sp, g_mem, w_mk, w_mv, g_mqn, g_mkn, w_br, w_o, g_ffn, w_rg, b_rg, w_re, b_re, w_eg, w_eu, w_ed):
    w = dict(g_mix=g_mix, w_in=w_in, g_qa=g_qa, w_qb=w_qb, g_kva=g_kva, w_kvb=w_kvb, g_qn=g_qn, g_kn=g_kn,
             w_pool=w_pool, pool_scale=pool_scale, g_sgv=g_sgv, w_sp=w_sp, b_sp=b_sp, g_mem=g_mem,
             w_mk=w_mk, w_mv=w_mv, g_mqn=g_mqn, g_mkn=g_mkn, w_br=w_br, w_o=w_o, g_ffn=g_ffn,
             w_rg=w_rg, b_rg=b_rg, w_re=w_re, b_re=b_re, w_eg=w_eg, w_eu=w_eu, w_ed=w_ed)
    nb, seq, _ = x_prompt.shape
    ndb, t_new, _ = x_sample.shape
    depth = w_in.shape[0]
    past = page_table.shape[1] * PAGE

    cos_p, sin_p = _rope_tables(jnp.arange(seq, dtype=jnp.int32))
    cos_s, sin_s = _rope_tables(past + jnp.arange(t_new, dtype=jnp.int32))
    tm_s = _tile(ndb * t_new, 512)
    cos_s = jnp.tile(cos_s, (tm_s // t_new, 1))
    sin_s = jnp.tile(sin_s, (tm_s // t_new, 1))

    xp = x_prompt.reshape(nb * seq, D_MODEL)
    xs = x_sample.reshape(ndb * t_new, D_MODEL)
    mem = mem_prompt.reshape(nb * N_MEM, D_MODEL)
    zero_buf = jnp.zeros((nb, POOL_PAD, POOL_W), F32)
    memk_s = cache_memk.reshape(depth, ndb, N_MEM * MEM_HEADS, MEM_HD)
    memv_s = cache_memv.reshape(depth, ndb, N_MEM * MEM_HEADS, MEM_HD)
    cache_kpe_t = jnp.swapaxes(cache_kpe, 2, 3)

    outs = [[] for _ in range(9)]
    for l in range(depth):
        lw = _layer_weights(l, w)
        mk, mv = _memkv(mem, lw["g_mem"], lw["wmkv"], lw["g_mkn"], hp=True)
        xp, c, kp, pb, _ = _group_layer(
            xp, lw, nseq=nb, seqlen=seq, pos0=0, cos=cos_p, sin=sin_p, pool_buf=zero_buf,
            mem_k=mk, mem_v=mv, kv_index=lambda b: (b, 0), mla_sample_fn=None, hp=True)
        outs[0].append(c.reshape(nb, seq, KV_LORA))
        outs[1].append(kp.reshape(nb, seq, QK_ROPE))
        outs[2].append(mk.reshape(nb, N_MEM, MEM_HEADS, MEM_HD))
        outs[3].append(mv.reshape(nb, N_MEM, MEM_HEADS, MEM_HD))
        outs[4].append(pb[:, 1:])

        def mla_sample_fn(q, c_new, kpe_new, l=l, lw=lw):
            o = _mla_sample(q.reshape(ndb, t_new, -1), c_new.reshape(ndb, t_new, KV_LORA),
                            kpe_new.reshape(ndb, t_new, QK_ROPE), cache_ckv, cache_kpe_t, page_table, l, lw)
            return o.reshape(ndb * t_new, N_HEADS * V_HEAD)

        buf_s = jnp.pad(state_pool[l], ((0, 0), (1, 0), (0, 0)))
        xs, c, kp, pb, vr = _group_layer(
            xs, lw, nseq=ndb, seqlen=t_new, pos0=past, cos=cos_s, sin=sin_s, pool_buf=buf_s,
            mem_k=memk_s, mem_v=memv_s, kv_index=lambda b, l=l: (l, b, 0, 0),
            mla_sample_fn=mla_sample_fn, hp=False)
        outs[5].append(c.reshape(ndb, t_new, KV_LORA))
        outs[6].append(kp.reshape(ndb, t_new, QK_ROPE))
        outs[7].append(pb[:, 1:])
        outs[8].append(vr.reshape(ndb, t_new, SG_W))

    stacked = [jnp.stack(o) for o in outs]
    return (xp.reshape(nb, seq, D_MODEL), xs.reshape(ndb, t_new, D_MODEL), *stacked)
```
